```python
import jax
import jax.numpy as jnp
from jax import lax
import numpy as np

D_MODEL = 1024
BATCH = 8
SEQ = 4096
DEPTH = 4

GRID_W = 64
CTX_LEN = 256
N_MIXERS = 4
ROPE_BASE = 10000.0
NORM_EPS = 1e-6
NEG_INF = -1e30
Q_BLOCK = 128

NA_HEADS = 16
NA_HEAD_DIM = D_MODEL // NA_HEADS
NA_WIN_ROWS = 8
NA_WIN_COLS = 16

RW_HEAD = 64
RW_HEADS = D_MODEL // RW_HEAD
RW_DECAY_LORA = 64
RW_ICLR_LORA = 64
RW_GATE_LORA = 128
RW_GN_EPS = 64e-5

MLA_HEADS = 16
MLA_Q_RANK = 384
MLA_KV_RANK = 256
MLA_NOPE = 64
MLA_ROPE = 32
MLA_V = 64

SWA_Q_HEADS = 16
SWA_KV_HEADS = 4
SWA_GROUP = SWA_Q_HEADS // SWA_KV_HEADS
SWA_HEAD_DIM = 64
SWA_WINDOW = 128

N_EXPERTS = 16
EC_CAPACITY = 2
D_FF_EXPERT = 1024

kernel_name = 'hybrid_dit_na_rwkv7_mla_swa_ecmoe'


def rms_norm(x, g):
    xf = x.astype(jnp.float32)
    y = xf * lax.rsqrt(jnp.mean(xf * xf, axis=-1, keepdims=True) + NORM_EPS)
    return (y * g.astype(jnp.float32)).astype(x.dtype)


def modulate(h, shift, scale):
    return h * (1 + scale) + shift


def ada_modulation(cond, w, b):
    m = jax.nn.silu(cond) @ w + b
    return jnp.split(m, 6, axis=-1)


def axial_rope(n, d_rot):
    t = jnp.arange(n)
    row = (t // GRID_W).astype(jnp.float32)
    col = (t % GRID_W).astype(jnp.float32)
    d_axis = d_rot // 2
    inv = ROPE_BASE ** (-jnp.arange(0, d_axis, 2, dtype=jnp.float32) / d_axis)
    ang = jnp.concatenate([row[:, None] * inv, col[:, None] * inv], axis=-1)
    return jnp.cos(ang), jnp.sin(ang)


def apply_rope(x, cos, sin):
    half = x.shape[-1] // 2
    xf = x.astype(jnp.float32)
    x1, x2 = xf[..., :half], xf[..., half:]
    return jnp.concatenate([x1 * cos - x2 * sin, x2 * cos + x1 * sin], axis=-1).astype(x.dtype)


def context_attention(q, k, v, scale, sink=None):
    L = k.shape[1]
    s = jnp.einsum('bqhgd,bkhd->bhgqk', q, k).astype(jnp.float32) * scale
    if sink is not None:
        snk = jnp.broadcast_to(sink.astype(jnp.float32)[None, :, :, None, None], s.shape[:-1] + (1,))
        s = jnp.concatenate([s, snk], axis=-1)
    p = jax.nn.softmax(s, axis=-1)[..., :L]
    return jnp.einsum('bhgqk,bkhd->bqhgd', p.astype(v.dtype), v)


def neighbourhood_attention(hc, hl, w_qkv, q_g, k_g, rpb, w_o, ctx_out):
    B, n, D = hl.shape
    L = hc.shape[1]
    rows = n // GRID_W
    H, dh = NA_HEADS, NA_HEAD_DIM
    scale = dh ** -0.5

    def project(h):
        qkv = (h @ w_qkv).reshape(h.shape[0], h.shape[1], 3, H, dh)
        return rms_norm(qkv[:, :, 0], q_g), rms_norm(qkv[:, :, 1], k_g), qkv[:, :, 2]

    qc, kc, vc = project(hc)
    ql, kl, vl = project(hl)

    kr = min(NA_WIN_ROWS, rows)
    n_cb = GRID_W // NA_WIN_COLS
    strip = 2 * NA_WIN_COLS
    qcol = np.arange(GRID_W).reshape(n_cb, NA_WIN_COLS)
    strip0 = np.clip(np.arange(n_cb) * NA_WIN_COLS - NA_WIN_COLS // 2, 0, GRID_W - strip)
    kcol = strip0[:, None] + np.arange(strip)
    cstart = np.clip(qcol - NA_WIN_COLS // 2, 0, GRID_W - NA_WIN_COLS)
    kc3 = kcol[:, None, :]
    col_ok = (kc3 >= cstart[:, :, None]) & (kc3 < cstart[:, :, None] + NA_WIN_COLS)
    dcol_idx = np.clip(kc3 - qcol[:, :, None] + NA_WIN_COLS - 1, 0, 2 * NA_WIN_COLS - 2)
    rpb_cols = rpb[:, :, dcol_idx]
    col_ok = jnp.asarray(col_ok)[:, :, None, :]

    qg = ql.reshape(B, rows, GRID_W, H, dh)
    kg = kl.reshape(B, rows, GRID_W, H, dh)
    vg = vl.reshape(B, rows, GRID_W, H, dh)

    def row_step(r):
        r0 = jnp.clip(r - kr // 2, 0, rows - kr)
        kw = lax.dynamic_slice_in_dim(kg, r0, kr, axis=1)[:, :, kcol]
        vw = lax.dynamic_slice_in_dim(vg, r0, kr, axis=1)[:, :, kcol]
        q = lax.dynamic_index_in_dim(qg, r, axis=1, keepdims=False)[:, qcol]
        dr_idx = r0 + jnp.arange(kr) - r + (NA_WIN_ROWS - 1)
        bias = jnp.transpose(rpb_cols[:, dr_idx], (0, 2, 3, 1, 4)).astype(jnp.float32)
        s_lat = jnp.einsum('bjqhd,brjkhd->bhjqrk', q, kw).astype(jnp.float32) * scale + bias[None]
        s_lat = jnp.where(col_ok, s_lat, NEG_INF).reshape(B, H, n_cb, NA_WIN_COLS, kr * strip)
        s_ctx = jnp.einsum('bjqhd,bkhd->bhjqk', q, kc).astype(jnp.float32) * scale
        p = jax.nn.softmax(jnp.concatenate([s_lat, s_ctx], axis=-1), axis=-1).astype(vl.dtype)
        p_lat = p[..., :kr * strip].reshape(B, H, n_cb, NA_WIN_COLS, kr, strip)
        o = (jnp.einsum('bhjqrk,brjkhd->bjqhd', p_lat, vw)
             + jnp.einsum('bhjqk,bkhd->bjqhd', p[..., kr * strip:], vc))
        return o.reshape(B, GRID_W, H, dh)

    o = lax.map(row_step, jnp.arange(rows))
    ol = jnp.moveaxis(o, 0, 1).reshape(B, n, H * dh) @ w_o
    oc = None
    if ctx_out:
        oc = context_attention(qc[:, :, :, None], kc, vc, scale).reshape(B, L, H * dh) @ w_o
    return oc, ol


def centred_shift(h):
    hp = jnp.pad(h, ((0, 0), (1, 1), (0, 0)))
    return 0.5 * (hp[:, :-2] + hp[:, 2:])


def _rwkv_step(S, inp):
    r, w, kk, kka, kt, v = inp
    sk = jnp.einsum('bhij,bhj->bhi', S, kk)
    S = S * w[:, :, None, :] - sk[..., None] * kka[:, :, None, :] + v[..., None] * kt[:, :, None, :]
    return S, jnp.einsum('bhij,bhj->bhi', S, r)


def _rwkv_scan(S0, seqs, reverse):
    xs = tuple(jnp.swapaxes(t, 0, 1) for t in seqs)
    S, y = lax.scan(_rwkv_step, S0, xs, reverse=reverse)
    return S, jnp.swapaxes(y, 0, 1)


def rwkv7_bidirectional(hc, hl, mix, w_r, w_k, w_v, w0, w1, w2, a0, a1, a2, g1, g2,
                        k_k, k_a, r_k, ln_g, ln_b, w_o, ctx_out):
    f32 = jnp.float32
    H, N = RW_HEADS, RW_HEAD

    def heads(t):
        return t.reshape(t.shape[:-1] + (H, N))

    def stream(h, S0):
        B = h.shape[0]
        xx = centred_shift(h) - h
        xr, xw, xk, xv, xa, xg = (h + xx * mix[s] for s in range(6))
        r = heads((xr @ w_r).astype(f32))
        k = (xk @ w_k).astype(f32)
        v = heads((xv @ w_v).astype(f32))
        g = jax.nn.sigmoid(xg @ g1) @ g2
        dec = w0[:, None, None, :] + jnp.einsum('zbtr,zrd->zbtd', jnp.tanh(jnp.einsum('btd,zdr->zbtr', xw, w1)), w2)
        logw = -jax.nn.softplus(-dec.astype(f32)) - 0.5
        w = heads(jnp.exp(-jnp.exp(logw)))
        a = heads(jax.nn.sigmoid((a0[:, None, None, :] + jnp.einsum(
            'zbtr,zrd->zbtd', jnp.einsum('btd,zdr->zbtr', xa, a1), a2)).astype(f32)))
        kk = heads(k * k_k.astype(f32))
        kk = kk / jnp.maximum(jnp.linalg.norm(kk, axis=-1, keepdims=True), 1e-12)
        kt = heads(k)[None] * (1 + (a - 1) * heads(k_a.astype(f32)))
        if S0 is None:
            z = jnp.zeros((B, H, N, N), f32)
            S0 = (z, z)
        S_f, y_f = _rwkv_scan(S0[0], (r, w[0], kk, kk * a[0], kt[0], v), False)
        S_b, y_b = _rwkv_scan(S0[1], (r, w[1], kk, kk * a[1], kt[1], v), True)
        return (S_f, S_b), (y_f + y_b, r, kt, v, g)

    def readout(y, r, kt, v, g, dtype):
        B, T = y.shape[:2]
        mu = jnp.mean(y, axis=-1, keepdims=True)
        var = jnp.mean(jnp.square(y - mu), axis=-1, keepdims=True)
        yn = ((y - mu) * lax.rsqrt(var + RW_GN_EPS)).reshape(B, T, -1) * ln_g.astype(f32) + ln_b.astype(f32)
        bonus = jnp.sum(jnp.sum(r[None] * kt * r_k.astype(f32), axis=-1, keepdims=True), axis=0) * v
        return ((yn + bonus.reshape(B, T, -1)).astype(dtype) * g) @ w_o

    ctx_state, ctx_feats = stream(hc, None)
    _, lat_feats = stream(hl, ctx_state)
    ol = readout(*lat_feats, hl.dtype)
    oc = readout(*ctx_feats, hc.dtype) if ctx_out else None
    return oc, ol


def multi_head_latent_attention(hc, hl, w_down, q_norm_g, kv_norm_g, w_uq, w_ukv,
                                qn_g, qr_g, kn_g, kr_g, w_o, cos, sin, ctx_out):
    f32 = jnp.float32
    H = MLA_HEADS
    scale = (MLA_NOPE + MLA_ROPE) ** -0.5

    def project(h, rope):
        B, T, _ = h.shape
        d = h @ w_down
        cq = rms_norm(d[..., :MLA_Q_RANK], q_norm_g)
        ckv = rms_norm(d[..., MLA_Q_RANK:MLA_Q_RANK + MLA_KV_RANK], kv_norm_g)
        k_r = rms_norm(d[..., MLA_Q_RANK + MLA_KV_RANK:], kr_g)
        q = (cq @ w_uq).reshape(B, T, H, MLA_NOPE + MLA_ROPE)
        kv = (ckv @ w_ukv).reshape(B, T, H, MLA_NOPE + MLA_V)
        q_n = rms_norm(q[..., :MLA_NOPE], qn_g)
        q_r = rms_norm(q[..., MLA_NOPE:], qr_g)
        k_n = rms_norm(kv[..., :MLA_NOPE], kn_g)
        v = kv[..., MLA_NOPE:]
        if rope:
            q_r = apply_rope(q_r, cos[:, None], sin[:, None])
            k_r = apply_rope(k_r, cos, sin)
        return q_n, q_r, k_n, k_r, v

    qn_l, qr_l, kn_l, kr_l, v_l = project(hl, True)
    qn_c, qr_c, kn_c, kr_c, v_c = project(hc, False)
    B, n = hl.shape[:2]
    L = hc.shape[1]

    def block(b):
        s0 = b * Q_BLOCK
        qn = lax.dynamic_slice_in_dim(qn_l, s0, Q_BLOCK, axis=1)
        qr = lax.dynamic_slice_in_dim(qr_l, s0, Q_BLOCK, axis=1)
        s_lat = jnp.einsum('bqhd,bkhd->bhqk', qn, kn_l) + jnp.einsum('bqhd,bkd->bhqk', qr, kr_l)
        s_ctx = jnp.einsum('bqhd,bkhd->bhqk', qn, kn_c) + jnp.einsum('bqhd,bkd->bhqk', qr, kr_c)
        s = jnp.concatenate([s_lat, s_ctx], axis=-1).astype(f32) * scale
        p = jax.nn.softmax(s, axis=-1).astype(v_l.dtype)
        return (jnp.einsum('bhqk,bkhd->bqhd', p[..., :n], v_l)
                + jnp.einsum('bhqk,bkhd->bqhd', p[..., n:], v_c))

    o = lax.map(block, jnp.arange(n // Q_BLOCK))
    ol = jnp.moveaxis(o, 0, 1).reshape(B, n, H * MLA_V) @ w_o
    oc = None
    if ctx_out:
        q = jnp.concatenate([qn_c, qr_c], axis=-1)
        k = jnp.concatenate([kn_c, jnp.broadcast_to(kr_c[:, :, None, :], (B, L, H, MLA_ROPE))], axis=-1)
        oc = context_attention(q[:, :, :, None], k, v_c, scale).reshape(B, L, H * MLA_V) @ w_o
    return oc, ol


def window_gqa_sink(hc, hl, w_qkv, q_g, k_g, sink, w_o, cos, sin, ctx_out):
    f32 = jnp.float32
    Hk, G, dh = SWA_KV_HEADS, SWA_GROUP, SWA_HEAD_DIM
    scale = dh ** -0.5
    sink_hg = sink.reshape(Hk, G)
    nq = Hk * G * dh

    def project(h, rope):
        B, T, _ = h.shape
        qkv = h @ w_qkv
        q = rms_norm(qkv[..., :nq].reshape(B, T, Hk, G, dh), q_g)
        k = rms_norm(qkv[..., nq:nq + Hk * dh].reshape(B, T, Hk, dh), k_g)
        v = qkv[..., nq + Hk * dh:].reshape(B, T, Hk, dh)
        if rope:
            q = apply_rope(q, cos[:, None, None], sin[:, None, None])
            k = apply_rope(k, cos[:, None], sin[:, None])
        return q, k, v

    ql, kl, vl = project(hl, True)
    qc, kc, vc = project(hc, False)
    B, n = hl.shape[:2]
    L = hc.shape[1]
    pad = ((0, 0), (Q_BLOCK, Q_BLOCK), (0, 0), (0, 0))
    kp, vp = jnp.pad(kl, pad), jnp.pad(vl, pad)
    q_off = jnp.arange(Q_BLOCK)
    k_off = jnp.arange(3 * Q_BLOCK) - Q_BLOCK

    def block(b):
        s0 = b * Q_BLOCK
        q = lax.dynamic_slice_in_dim(ql, s0, Q_BLOCK, axis=1)
        k = lax.dynamic_slice_in_dim(kp, s0, 3 * Q_BLOCK, axis=1)
        v = lax.dynamic_slice_in_dim(vp, s0, 3 * Q_BLOCK, axis=1)
        qpos, kpos = s0 + q_off, s0 + k_off
        ok = ((kpos >= 0) & (kpos < n))[None, :] & (jnp.abs(qpos[:, None] - kpos[None, :]) <= SWA_WINDOW)
        s_loc = jnp.where(ok, jnp.einsum('bqhgd,bkhd->bhgqk', q, k).astype(f32) * scale, NEG_INF)
        s_ctx = jnp.einsum('bqhgd,bkhd->bhgqk', q, kc).astype(f32) * scale
        s_snk = jnp.broadcast_to(sink_hg.astype(f32)[None, :, :, None, None], s_loc.shape[:-1] + (1,))
        p = jax.nn.softmax(jnp.concatenate([s_loc, s_ctx, s_snk], axis=-1), axis=-1).astype(vl.dtype)
        return (jnp.einsum('bhgqk,bkhd->bqhgd', p[..., :3 * Q_BLOCK], v)
                + jnp.einsum('bhgqk,bkhd->bqhgd', p[..., 3 * Q_BLOCK:-1], vc))

    o = lax.map(block, jnp.arange(n // Q_BLOCK))
    ol = jnp.moveaxis(o, 0, 1).reshape(B, n, nq) @ w_o
    oc = None
    if ctx_out:
        oc = context_attention(qc, kc, vc, scale, sink_hg).reshape(B, L, nq) @ w_o
    return oc, ol


def expert_choice_ffn(h, w_router, w1, w3, w2):
    B, T, D = h.shape
    cap = max(1, EC_CAPACITY * T // N_EXPERTS)
    aff = jax.nn.softmax((h @ w_router).astype(jnp.float32), axis=-1)
    gate, idx = lax.top_k(jnp.swapaxes(aff, 1, 2), cap)
    xin = jax.vmap(lambda hb, ib: hb[ib])(h, idx)
    hid = jax.nn.silu(jnp.einsum('becd,edf->becf', xin, w1)) * jnp.einsum('becd,edf->becf', xin, w3)
    y = jnp.einsum('becf,efd->becd', hid, w2) * gate[..., None].astype(h.dtype)
    return jax.vmap(lambda yb, ib: jax.ops.segment_sum(
        yb.reshape(-1, D), ib.reshape(-1), num_segments=T))(y, idx)


def _layers_of(mixer):
    return len(range(mixer, DEPTH, N_MIXERS))


def setup_inputs(seed: int = 0) -> dict:
    key = jax.random.key(seed)
    keys = iter(jax.random.split(key, 96))
    f32 = jnp.float32
    D = D_MODEL

    def normal(shape, scale):
        return jax.random.normal(next(keys), shape, f32) * scale

    def gain(shape):
        return 1.0 + 0.1 * jax.random.normal(next(keys), shape, f32)

    def uniform(shape, lo, hi):
        return jax.random.uniform(next(keys), shape, f32, lo, hi)

    nA, nB, nC, nD = (_layers_of(m) for m in range(N_MIXERS))
    swa_cols = (SWA_Q_HEADS + 2 * SWA_KV_HEADS) * SWA_HEAD_DIM
    return {
        'x': normal((BATCH, SEQ, D), 1.0),
        'c': normal((BATCH, D), 1.0),
        'ctx': normal((BATCH, CTX_LEN, D), 1.0),
        'c_ctx': normal((D,), 1.0),
        'norm1_g': gain((DEPTH, D)),
        'norm2_g': gain((DEPTH, D)),
        'ada_w': normal((DEPTH, D, 6 * D), 0.5 * D ** -0.5),
        'ada_b': normal((DEPTH, 6 * D), 0.02),
        'na_w_qkv': normal((nA, D, 3 * D), D ** -0.5),
        'na_q_g': gain((nA, NA_HEAD_DIM)),
        'na_k_g': gain((nA, NA_HEAD_DIM)),
        'na_rpb': normal((nA, NA_HEADS, 2 * NA_WIN_ROWS - 1, 2 * NA_WIN_COLS - 1), 0.5),
        'na_w_o': normal((nA, D, D), D ** -0.5),
        'rw_mix': uniform((nB, 6, D), 0.0, 1.0),
        'rw_w_r': normal((nB, D, D), D ** -0.5),
        'rw_w_k': normal((nB, D, D), D ** -0.5),
        'rw_w_v': normal((nB, D, D), D ** -0.5),
        'rw_w0': uniform((nB, 2, D), -6.0, -1.0),
        'rw_w1': normal((nB, 2, D, RW_DECAY_LORA), D ** -0.5),
        'rw_w2': normal((nB, 2, RW_DECAY_LORA, D), 0.5 * RW_DECAY_LORA ** -0.5),
        'rw_a0': normal((nB, 2, D), 0.5),
        'rw_a1': normal((nB, 2, D, RW_ICLR_LORA), D ** -0.5),
        'rw_a2': normal((nB, 2, RW_ICLR_LORA, D), 0.5 * RW_ICLR_LORA ** -0.5),
        'rw_g1': normal((nB, D, RW_GATE_LORA), D ** -0.5),
        'rw_g2': normal((nB, RW_GATE_LORA, D), RW_GATE_LORA ** -0.5),
        'rw_k_k': uniform((nB, D), 0.7, 1.0),
        'rw_k_a': uniform((nB, D), 0.8, 1.2),
        'rw_r_k': normal((nB, RW_HEADS, RW_HEAD), 0.1),
        'rw_ln_g': gain((nB, D)),
        'rw_ln_b': normal((nB, D), 0.02),
        'rw_w_o': normal((nB, D, D), D ** -0.5),
        'mla_w_down': normal((nC, D, MLA_Q_RANK + MLA_KV_RANK + MLA_ROPE), D ** -0.5),
        'mla_q_norm_g': gain((nC, MLA_Q_RANK)),
        'mla_kv_norm_g': gain((nC, MLA_KV_RANK)),
        'mla_w_uq': normal((nC, MLA_Q_RANK, MLA_HEADS * (MLA_NOPE + MLA_ROPE)), MLA_Q_RANK ** -0.5),
        'mla_w_ukv': normal((nC, MLA_KV_RANK, MLA_HEADS * (MLA_NOPE + MLA_V)), MLA_KV_RANK ** -0.5),
        'mla_qn_g': gain((nC, MLA_NOPE)),
        'mla_qr_g': gain((nC, MLA_ROPE)),
        'mla_kn_g': gain((nC, MLA_NOPE)),
        'mla_kr_g': gain((nC, MLA_ROPE)),
        'mla_w_o': normal((nC, MLA_HEADS * MLA_V, D), (MLA_HEADS * MLA_V) ** -0.5),
        'swa_w_qkv': normal((nD, D, swa_cols), D ** -0.5),
        'swa_q_g': gain((nD, SWA_HEAD_DIM)),
        'swa_k_g': gain((nD, SWA_HEAD_DIM)),
        'swa_sink': normal((nD, SWA_Q_HEADS), 0.5),
        'swa_w_o': normal((nD, SWA_Q_HEADS * SWA_HEAD_DIM, D), (SWA_Q_HEADS * SWA_HEAD_DIM) ** -0.5),
        'moe_router': normal((DEPTH, D, N_EXPERTS), D ** -0.5),
        'moe_w1': normal((DEPTH, N_EXPERTS, D, D_FF_EXPERT), D ** -0.5),
        'moe_w3': normal((DEPTH, N_EXPERTS, D, D_FF_EXPERT), D ** -0.5),
        'moe_w2': normal((DEPTH, N_EXPERTS, D_FF_EXPERT, D), D_FF_EXPERT ** -0.5),
    }


def reference(x, c, ctx, c_ctx, norm1_g, norm2_g, ada_w, ada_b,
              na_w_qkv, na_q_g, na_k_g, na_rpb, na_w_o,
              rw_mix, rw_w_r, rw_w_k, rw_w_v, rw_w0, rw_w1, rw_w2, rw_a0, rw_a1, rw_a2,
              rw_g1, rw_g2, rw_k_k, rw_k_a, rw_r_k, rw_ln_g, rw_ln_b, rw_w_o,
              mla_w_down, mla_q_norm_g, mla_kv_norm_g, mla_w_uq, mla_w_ukv,
              mla_qn_g, mla_qr_g, mla_kn_g, mla_kr_g, mla_w_o,
              swa_w_qkv, swa_q_g, swa_k_g, swa_sink, swa_w_o,
              moe_router, moe_w1, moe_w3, moe_w2):
    n = x.shape[1]
    cos_mla, sin_mla = axial_rope(n, MLA_ROPE)
    cos_swa, sin_swa = axial_rope(n, SWA_HEAD_DIM)
    xl, xc = x, ctx
    for i in range(DEPTH):
        mixer, j = i % N_MIXERS, i // N_MIXERS
        ctx_needed = i < DEPTH - 1
        sh1, sc1, gt1, sh2, sc2, gt2 = (t[:, None, :] for t in ada_modulation(c, ada_w[i], ada_b[i]))
        csh1, csc1, cgt1, csh2, csc2, cgt2 = ada_modulation(c_ctx, ada_w[i], ada_b[i])
        hl = modulate(rms_norm(xl, norm1_g[i]), sh1, sc1)
        hc = modulate(rms_norm(xc, norm1_g[i]), csh1, csc1)
        if mixer == 0:
            oc, ol = neighbourhood_attention(hc, hl, na_w_qkv[j], na_q_g[j], na_k_g[j], na_rpb[j],
                                             na_w_o[j], ctx_needed)
        elif mixer == 1:
            oc, ol = rwkv7_bidirectional(hc, hl, rw_mix[j], rw_w_r[j], rw_w_k[j], rw_w_v[j],
                                         rw_w0[j], rw_w1[j], rw_w2[j], rw_a0[j], rw_a1[j], rw_a2[j],
                                         rw_g1[j], rw_g2[j], rw_k_k[j], rw_k_a[j], rw_r_k[j],
                                         rw_ln_g[j], rw_ln_b[j], rw_w_o[j], ctx_needed)
        elif mixer == 2:
            oc, ol = multi_head_latent_attention(hc, hl, mla_w_down[j], mla_q_norm_g[j], mla_kv_norm_g[j],
                                                 mla_w_uq[j], mla_w_ukv[j], mla_qn_g[j], mla_qr_g[j],
                                                 mla_kn_g[j], mla_kr_g[j], mla_w_o[j],
                                                 cos_mla, sin_mla, ctx_needed)
        else:
            oc, ol = window_gqa_sink(hc, hl, swa_w_qkv[j], swa_q_g[j], swa_k_g[j], swa_sink[j],
                                     swa_w_o[j], cos_swa, sin_swa, ctx_needed)
        xl = xl + gt1 * ol
        hl = modulate(rms_norm(xl, norm2_g[i]), sh2, sc2)
        xl = xl + gt2 * expert_choice_ffn(hl, moe_router[i], moe_w1[i], moe_w3[i], moe_w2[i])
        if ctx_needed:
            xc = xc + cgt1 * oc
            hc = modulate(rms_norm(xc, norm2_g[i]), csh2, csc2)
            xc = xc + cgt2 * expert_choice_ffn(hc, moe_router[i], moe_w1[i], moe_w3[i], moe_w2[i])
    return xl
```

```python
import functools

import numpy as np
import jax
import jax.numpy as jnp
from jax import lax
from jax.experimental import pallas as pl
from jax.experimental.pallas import tpu as pltpu

F32 = jnp.float32
BF16 = jnp.bfloat16
HIGHEST = lax.Precision.HIGHEST

GRID_W = 64
ROPE_BASE = 10000.0
NORM_EPS = 1e-6
NEG_INF = -1e30
NA_HEADS = 16
NA_WIN_ROWS = 8
NA_WIN_COLS = 16
RW_HEAD = 64
RW_GN_EPS = 64e-5
MLA_HEADS = 16
MLA_Q_RANK = 384
MLA_KV_RANK = 256
MLA_NOPE = 64
MLA_ROPE = 32
MLA_V = 64
SWA_Q_HEADS = 16
SWA_KV_HEADS = 4
SWA_HEAD_DIM = 64
SWA_WINDOW = 128
N_EXPERTS = 16
EC_CAPACITY = 2

LANES = 128
TM = 256
VMEM_BIG = 56 * 1024 * 1024
VMEM_MID = 40 * 1024 * 1024


def _cparams(sem, vmem=VMEM_MID):
    return pltpu.CompilerParams(dimension_semantics=sem, vmem_limit_bytes=vmem)


def _tok_spec(width, tm=TM):
    return pl.BlockSpec((1, tm, width), lambda b, t: (b, t, 0))


def _const_spec(shape):
    nd = len(shape)
    return pl.BlockSpec(shape, lambda b, t: (0,) * nd)


def _mod_spec(d):
    return pl.BlockSpec((1, 1, 6, d), lambda b, t: (b, jnp.minimum(t, 1), 0, 0))


def _sigmoid(x):
    return 1.0 / (1.0 + jnp.exp(-x))


def _modnorm(x, g, shift, scale):
    ms = jnp.mean(x * x, axis=-1, keepdims=True)
    y = x * lax.rsqrt(ms + NORM_EPS) * g
    return y * (1.0 + scale) + shift


def _group_sums(x2, gmat):
    hi = x2.astype(BF16)
    lo = (x2 - hi.astype(F32)).astype(BF16)
    return (jnp.dot(hi, gmat, preferred_element_type=F32)
            + jnp.dot(lo, gmat, preferred_element_type=F32))


def _group_rms(x, gmat, inv_cnt, eps):
    outs = []
    for c in range(x.shape[1] // LANES):
        xc = x[:, c * LANES:(c + 1) * LANES]
        ss = _group_sums(xc * xc, gmat)
        outs.append(xc * lax.rsqrt(ss * inv_cnt + eps))
    return outs[0] if len(outs) == 1 else jnp.concatenate(outs, axis=1)


def _rope_swap(x, half):
    w = x.shape[1]
    lane = lax.broadcasted_iota(jnp.int32, x.shape, 1)
    fwd = pltpu.roll(x, w - half, 1)
    bwd = pltpu.roll(x, half, 1)
    return jnp.where((lane % (2 * half)) < half, fwd, bwd)


def _block_ones(sizes):
    m = np.zeros((LANES, LANES), np.float32)
    o = 0
    for s, on in sizes:
        if on:
            m[o:o + s, o:o + s] = 1.0
        o += s
    return jnp.asarray(m, BF16)


def _ada_kernel(cond_ref, w_ref, b_ref, o_ref):
    cnd = cond_ref[...]
    s = cnd * _sigmoid(cnd)
    o_ref[0] = jnp.dot(s, w_ref[0], preferred_element_type=F32, precision=HIGHEST) + b_ref[0]


def _ada_all(c, c_ctx, ada_w, ada_b):
    depth, d, d6 = ada_w.shape
    bsz = c.shape[0]
    rows = 16
    cond = jnp.zeros((rows, d), F32).at[:bsz].set(c).at[bsz].set(c_ctx)
    tn = 1536
    m = pl.pallas_call(
        _ada_kernel,
        out_shape=jax.ShapeDtypeStruct((depth, rows, d6), F32),
        grid=(depth, d6 // tn),
        in_specs=[pl.BlockSpec((rows, d), lambda i, j: (0, 0)),
                  pl.BlockSpec((1, d, tn), lambda i, j: (i, 0, j)),
                  pl.BlockSpec((1, 1, tn), lambda i, j: (i, 0, j))],
        out_specs=pl.BlockSpec((1, rows, tn), lambda i, j: (i, 0, j)),
        compiler_params=_cparams(("parallel", "parallel")),
        name="ada_mod",
    )(cond, ada_w, ada_b.reshape(depth, 1, d6))
    lat = m[:, :bsz].reshape(depth, bsz, 6, d)
    ctx = jnp.broadcast_to(m[:, bsz].reshape(depth, 1, 6, d), (depth, bsz, 6, d))
    return jnp.stack([ctx, lat], axis=2)


def _oproj_kernel(x_ref, o_ref, mod_ref, w_ref, out_ref, *, gate_row):
    y = jnp.dot(o_ref[0], w_ref[...], preferred_element_type=F32)
    out_ref[0] = x_ref[0] + mod_ref[0, 0, gate_row:gate_row + 1, :] * y


def _oproj(x, o, mod, w_bf16, gate_row):
    bsz, t, d = x.shape
    din = o.shape[2]
    return pl.pallas_call(
        functools.partial(_oproj_kernel, gate_row=gate_row),
        out_shape=jax.ShapeDtypeStruct(x.shape, F32),
        grid=(bsz, t // TM),
        in_specs=[_tok_spec(d), _tok_spec(din), _mod_spec(d), _const_spec((din, d))],
        out_specs=_tok_spec(d),
        compiler_params=_cparams(("parallel", "parallel")),
        name="oproj",
    )(x, o, mod, w_bf16)


def _na_proj_kernel(x_ref, g_ref, mod_ref, w_ref, gq_ref, gk_ref, gm_ref, q_ref, k_ref, v_ref, *, scale):
    d = x_ref.shape[2]
    hn = _modnorm(x_ref[0], g_ref[...], mod_ref[0, 0, 0:1, :], mod_ref[0, 0, 1:2, :]).astype(BF16)
    qkv = jnp.dot(hn, w_ref[...], preferred_element_type=F32)
    inv = 1.0 / (d // NA_HEADS)
    q = _group_rms(qkv[:, :d], gm_ref[...], inv, NORM_EPS) * (gq_ref[...] * scale)
    k = _group_rms(qkv[:, d:2 * d], gm_ref[...], inv, NORM_EPS) * gk_ref[...]
    q_ref[0] = q.astype(BF16)
    k_ref[0] = k.astype(BF16)
    v_ref[0] = qkv[:, 2 * d:].astype(BF16)


def _na_attn_kernel(q_ref, k_ref, v_ref, bias_ref, o_ref, *, ctx_len, rows, qblk_per_ctx):
    j = pl.program_id(1)
    r = jnp.maximum(j - qblk_per_ctx, 0)
    r0 = jnp.clip(r - NA_WIN_ROWS // 2, 0, rows - NA_WIN_ROWS)
    ws = pl.multiple_of(ctx_len + GRID_W * r0, GRID_W)
    nwin = NA_WIN_ROWS * GRID_W
    lane = lax.broadcasted_iota(jnp.int32, (GRID_W, LANES), 1)
    dn = (((1,), (1,)), ((), ()))
    for g in range(q_ref.shape[2] // LANES):
        cs = slice(g * LANES, (g + 1) * LANES)
        q2 = q_ref[0, :, cs]
        kw = k_ref[0, pl.ds(ws, nwin), cs]
        vw = v_ref[0, pl.ds(ws, nwin), cs]
        kc = k_ref[0, 0:ctx_len, cs]
        vc = v_ref[0, 0:ctx_len, cs]
        outs = []
        for a in range(2):
            qa = jnp.where((lane < LANES // 2) == (a == 0), q2, jnp.zeros_like(q2))
            s1 = lax.dot_general(qa, kw, dn, preferred_element_type=F32) + bias_ref[0, 2 * g + a]
            s2 = lax.dot_general(qa, kc, dn, preferred_element_type=F32)
            m = jnp.maximum(jnp.max(s1, axis=-1, keepdims=True), jnp.max(s2, axis=-1, keepdims=True))
            p1 = jnp.exp(s1 - m)
            p2 = jnp.exp(s2 - m)
            den = jnp.sum(p1, axis=-1, keepdims=True) + jnp.sum(p2, axis=-1, keepdims=True)
            o = (jnp.dot(p1.astype(BF16), vw, preferred_element_type=F32)
                 + jnp.dot(p2.astype(BF16), vc, preferred_element_type=F32))
            outs.append(o / den)
        o_ref[0, :, cs] = jnp.where(lane < LANES // 2, outs[0], outs[1]).astype(BF16)


def _na_bias_table(rpb):
    h = rpb.shape[0]
    kr = np.arange(NA_WIN_ROWS)
    dr = kr[None, :] - np.arange(NA_WIN_ROWS)[:, None] + (NA_WIN_ROWS - 1)
    col = np.arange(GRID_W)
    dc = np.clip(col[None, :] - col[:, None] + NA_WIN_COLS - 1, 0, 2 * NA_WIN_COLS - 2)
    cstart = np.clip(col - NA_WIN_COLS // 2, 0, GRID_W - NA_WIN_COLS)
    ok = (col[None, :] >= cstart[:, None]) & (col[None, :] < cstart[:, None] + NA_WIN_COLS)
    t = rpb[:, dr[:, None, :, None], dc[None, :, None, :]]
    t = jnp.where(jnp.asarray(ok)[None, None, :, None, :], t.astype(F32), NEG_INF)
    t = jnp.transpose(t, (1, 0, 2, 3, 4)).reshape(NA_WIN_ROWS, h, GRID_W, NA_WIN_ROWS * GRID_W)
    return jnp.concatenate([t, jnp.full((1,) + t.shape[1:], NEG_INF, F32)], axis=0)


def _layer_na(x, mod, ctx_len, norm_g, w_qkv, q_g, k_g, rpb, w_o):
    bsz, t, d = x.shape
    n = t - ctx_len
    rows = n // GRID_W
    dh = d // NA_HEADS
    gm = _block_ones([(dh, True), (dh, True)])
    q, k, v = pl.pallas_call(
        functools.partial(_na_proj_kernel, scale=dh ** -0.5),
        out_shape=[jax.ShapeDtypeStruct((bsz, t, d), BF16)] * 3,
        grid=(bsz, t // TM),
        in_specs=[_tok_spec(d), _const_spec((1, d)), _mod_spec(d), _const_spec((d, 3 * d)),
                  _const_spec((1, d)), _const_spec((1, d)), _const_spec((LANES, LANES))],
        out_specs=[_tok_spec(d)] * 3,
        compiler_params=_cparams(("parallel", "parallel")),
        name="na_proj",
    )(x, norm_g.reshape(1, d), mod, w_qkv.astype(BF16),
      jnp.tile(q_g, NA_HEADS).reshape(1, d), jnp.tile(k_g, NA_HEADS).reshape(1, d), gm)

    table = _na_bias_table(rpb)
    qpc = ctx_len // GRID_W

    def bias_idx(b, j):
        r = jnp.maximum(j - qpc, 0)
        delta = r - jnp.clip(r - NA_WIN_ROWS // 2, 0, rows - NA_WIN_ROWS)
        return (jnp.where(j < qpc, NA_WIN_ROWS, delta), 0, 0, 0)

    o = pl.pallas_call(
        functools.partial(_na_attn_kernel, ctx_len=ctx_len, rows=rows, qblk_per_ctx=qpc),
        out_shape=jax.ShapeDtypeStruct((bsz, t, d), BF16),
        grid=(bsz, t // GRID_W),
        in_specs=[pl.BlockSpec((1, GRID_W, d), lambda b, j: (b, j, 0)),
                  pl.BlockSpec((1, t, d), lambda b, j: (b, 0, 0)),
                  pl.BlockSpec((1, t, d), lambda b, j: (b, 0, 0)),
                  pl.BlockSpec((1, NA_HEADS, GRID_W, NA_WIN_ROWS * GRID_W), bias_idx)],
        out_specs=pl.BlockSpec((1, GRID_W, d), lambda b, j: (b, j, 0)),
        compiler_params=_cparams(("parallel", "arbitrary"), VMEM_BIG),
        name="na_attn",
    )(q, k, v, table)
    return _oproj(x, o, mod, w_o.astype(BF16), 2)


RW_CHUNK = 64


def _rw_proj_kernel(x_ref, xp_ref, xn_ref, g_ref, mod_ref, mix_ref, wr_ref, wk_ref, wv_ref, g1_ref, g2_ref,
                    w1_ref, w2_ref, w0_ref, a1_ref, a2_ref, a0_ref, kk_g_ref, ka_ref, gm_ref,
                    r_ref, v_ref, kk_ref, gt_ref, lw_ref, kka_ref, kt_ref):
    t = pl.program_id(1)
    nt = pl.num_programs(1)
    g = g_ref[...]
    shift, scale = mod_ref[0, 0, 0:1, :], mod_ref[0, 0, 1:2, :]
    h = _modnorm(x_ref[0], g, shift, scale)
    tm = h.shape[0]
    hp8 = _modnorm(xp_ref[0], g, shift, scale)
    hn8 = _modnorm(xn_ref[0], g, shift, scale)
    prev_row = jnp.where(t >= 2, hp8[7:8, :], 0.0)
    next_row = jnp.where(jnp.logical_and(t >= 1, t < nt - 1), hn8[0:1, :], 0.0)
    row = lax.broadcasted_iota(jnp.int32, h.shape, 0)
    h_prev = jnp.where(row == 0, prev_row, pltpu.roll(h, 1, 0))
    h_next = jnp.where(row == tm - 1, next_row, pltpu.roll(h, tm - 1, 0))
    xx = 0.5 * (h_prev + h_next) - h

    def mixed(s):
        return (h + xx * mix_ref[s:s + 1, :]).astype(BF16)

    r = jnp.dot(mixed(0), wr_ref[...], preferred_element_type=F32)
    k = jnp.dot(mixed(2), wk_ref[...], preferred_element_type=F32)
    v = jnp.dot(mixed(3), wv_ref[...], preferred_element_type=F32)
    gl = _sigmoid(jnp.dot(mixed(5), g1_ref[...], preferred_element_type=F32)).astype(BF16)
    gt_ref[0] = jnp.dot(gl, g2_ref[...], preferred_element_type=F32)
    dlo = jnp.tanh(jnp.dot(mixed(1), w1_ref[...], preferred_element_type=F32)).astype(BF16)
    alo = jnp.dot(mixed(4), a1_ref[...], preferred_element_type=F32).astype(BF16)
    kk = k * kk_g_ref[...]
    nrm = jnp.concatenate([jnp.sqrt(_group_sums(kk[:, c * LANES:(c + 1) * LANES] ** 2, gm_ref[...]))
                           for c in range(kk.shape[1] // LANES)], axis=1)
    kk = kk / jnp.maximum(nrm, 1e-12)
    r_ref[0] = r
    v_ref[0] = v
    kk_ref[0] = kk
    for z in range(2):
        dec = w0_ref[z:z + 1, :] + jnp.dot(dlo, w2_ref[z], preferred_element_type=F32)
        sp = jnp.maximum(-dec, 0.0) + jnp.log(1.0 + jnp.exp(-jnp.abs(dec)))
        lw_ref[z, 0] = -jnp.exp(-sp - 0.5)
        a = _sigmoid(a0_ref[z:z + 1, :] + jnp.dot(alo, a2_ref[z], preferred_element_type=F32))
        kka_ref[z, 0] = kk * a
        kt_ref[z, 0] = k * (1.0 + (a - 1.0) * ka_ref[...])


def _rw_scan_kernel(lw_ref, kk_ref, kka_ref, kt_ref, r_ref, v_ref, y_ref, s_ref, *, reverse):
    c = RW_CHUNK
    hd = RW_HEAD

    @pl.when(pl.program_id(2) == 0)
    def _():
        s_ref[...] = jnp.zeros(s_ref.shape, F32)

    def mm(a, b):
        return jnp.dot(a, b, preferred_element_type=F32, precision=HIGHEST)

    def mm_nt(a, b):
        return lax.dot_general(a, b, (((1,), (1,)), ((), ())), preferred_element_type=F32, precision=HIGHEST)

    rr = lax.broadcasted_iota(jnp.int32, (c, c), 0)
    cc = lax.broadcasted_iota(jnp.int32, (c, c), 1)
    before = (cc > rr) if reverse else (cc < rr)
    strict = before.astype(F32)
    incl = jnp.logical_or(before, cc == rr).astype(F32)
    eye = (cc == rr).astype(F32)

    lw = lw_ref[0, 0]
    cl = mm(incl, lw)
    e_in = jnp.exp(cl)
    e_inv = jnp.exp(-cl)
    pt = kk_ref[0] * jnp.exp(cl - lw)
    rt = r_ref[0] * e_in
    qt = kka_ref[0, 0] * e_inv
    kt = kt_ref[0, 0] * e_inv
    v2 = v_ref[0]
    last = 0 if reverse else c - 1
    g_last = e_in[last:last + 1, :]

    lane = lax.broadcasted_iota(jnp.int32, (c, 2 * hd), 1)
    s2 = s_ref[...]
    u2 = jnp.zeros((c, 2 * hd), F32)
    y2 = jnp.zeros((c, 2 * hd), F32)
    for a in range(2):
        own = (lane >= hd) if a else (lane < hd)
        pa = jnp.where(own, pt, 0.0)
        ra = jnp.where(own, rt, 0.0)
        va = jnp.where(own, v2, 0.0)
        a_pq = mm_nt(pa, qt) * strict
        a_pk = mm_nt(pa, kt) * strict
        a_rq = mm_nt(ra, qt) * incl
        a_rk = mm_nt(ra, kt) * incl
        npow = -a_pq
        inv = eye + npow
        for _ in range(5):
            npow = mm(npow, npow)
            inv = inv + mm(inv, npow)
        ua = mm(inv, mm_nt(pa, s2) + mm(a_pk, va))
        y2 = y2 + mm_nt(ra, s2) - mm(a_rq, ua) + mm(a_rk, va)
        u2 = u2 + ua
    y_ref[0] = y2
    ri = lax.broadcasted_iota(jnp.int32, (2 * hd, 2 * hd), 0)
    ci = lax.broadcasted_iota(jnp.int32, (2 * hd, 2 * hd), 1)
    same = (ri < hd) == (ci < hd)
    upd = s2 - mm(u2.T, qt) + mm(v2.T, kt)
    s_ref[...] = jnp.where(same, upd, 0.0) * g_last


def _rw_out_kernel(x_ref, yf_ref, yb_ref, r_ref, v_ref, gt_ref, kt_ref, mod_ref, lng_ref, lnb_ref, rk_ref,
                   gm_ref, wo_ref, out_ref):
    y = yf_ref[0] + yb_ref[0]
    r = r_ref[0]
    brk = (kt_ref[0, 0] + kt_ref[1, 0]) * r * rk_ref[...]
    inv = 1.0 / RW_HEAD
    cols = []
    for c in range(y.shape[1] // LANES):
        cs = slice(c * LANES, (c + 1) * LANES)
        yc = y[:, cs]
        mu = _group_sums(yc, gm_ref[...]) * inv
        dv = yc - mu
        var = _group_sums(dv * dv, gm_ref[...]) * inv
        yn = dv * lax.rsqrt(var + RW_GN_EPS) * lng_ref[:, cs] + lnb_ref[:, cs]
        bonus = _group_sums(brk[:, cs], gm_ref[...]) * v_ref[0, :, cs]
        cols.append(yn + bonus)
    o = (jnp.concatenate(cols, axis=1) * gt_ref[0]).astype(BF16)
    out_ref[0] = x_ref[0] + mod_ref[0, 0, 2:3, :] * jnp.dot(o, wo_ref[...], preferred_element_type=F32)


def _layer_rwkv(x, mod, ctx_len, norm_g, mix, w_r, w_k, w_v, w0, w1, w2, a0, a1, a2, g1, g2,
                k_k, k_a, r_k, ln_g, ln_b, w_o):
    bsz, t, d = x.shape
    hd = RW_HEAD
    lora_d, lora_a, lora_g = w1.shape[2], a1.shape[2], g1.shape[1]

    def cat_in(w):
        return jnp.concatenate([w[0], w[1]], axis=1).astype(BF16)

    def pad_out(w):
        z = jnp.zeros_like(w[0])
        return jnp.stack([jnp.concatenate([w[0], z], axis=0), jnp.concatenate([z, w[1]], axis=0)]).astype(BF16)

    gm = _block_ones([(hd, True), (hd, True)])
    nt = t // TM
    sub = TM // 8
    tok = jax.ShapeDtypeStruct((bsz, t, d), F32)
    tok2 = jax.ShapeDtypeStruct((2, bsz, t, d), F32)
    spec2 = pl.BlockSpec((2, 1, TM, d), lambda b, i: (0, b, i, 0))
    r, v, kk, gt, lw, kka, kt = pl.pallas_call(
        _rw_proj_kernel,
        out_shape=[tok, tok, tok, tok, tok2, tok2, tok2],
        grid=(bsz, nt),
        in_specs=[_tok_spec(d),
                  pl.BlockSpec((1, 8, d), lambda b, i: (b, jnp.maximum(i * sub - 1, 0), 0)),
                  pl.BlockSpec((1, 8, d), lambda b, i: (b, jnp.minimum((i + 1) * sub, nt * sub - 1), 0)),
                  _const_spec((1, d)), _mod_spec(d), _const_spec((6, d)),
                  _const_spec((d, d)), _const_spec((d, d)), _const_spec((d, d)),
                  _const_spec((d, lora_g)), _const_spec((lora_g, d)),
                  _const_spec((d, 2 * lora_d)), _const_spec((2, 2 * lora_d, d)), _const_spec((2, d)),
                  _const_spec((d, 2 * lora_a)), _const_spec((2, 2 * lora_a, d)), _const_spec((2, d)),
                  _const_spec((1, d)), _const_spec((1, d)), _const_spec((LANES, LANES))],
        out_specs=[_tok_spec(d)] * 4 + [spec2] * 3,
        compiler_params=_cparams(("parallel", "parallel"), VMEM_BIG),
        name="rw_proj",
    )(x, x, x, norm_g.reshape(1, d), mod, mix, w_r.astype(BF16), w_k.astype(BF16), w_v.astype(BF16),
      g1.astype(BF16), g2.astype(BF16), cat_in(w1), pad_out(w2), w0, cat_in(a1), pad_out(a2), a0,
      k_k.reshape(1, d), k_a.reshape(1, d), gm)

    nc = t // RW_CHUNK
    ncc = ctx_len // RW_CHUNK
    def scan_specs(z, reverse):
        def chunk(s):
            return jnp.where(s < ncc, ncc - 1 - s, nc - 1 - s + ncc) if reverse else s

        def dir_map(b, g, s):
            return (z, b, chunk(s), g)

        def sh_map(b, g, s):
            return (b, chunk(s), g)

        return (pl.BlockSpec((1, 1, RW_CHUNK, 2 * hd), dir_map), pl.BlockSpec((1, RW_CHUNK, 2 * hd), sh_map))

    ys = []
    for z, reverse in enumerate((False, True)):
        dir_spec, sh_spec = scan_specs(z, reverse)
        ys.append(pl.pallas_call(
            functools.partial(_rw_scan_kernel, reverse=reverse),
            out_shape=tok,
            grid=(bsz, d // (2 * hd), nc),
            in_specs=[dir_spec, sh_spec, dir_spec, dir_spec, sh_spec, sh_spec],
            out_specs=sh_spec,
            scratch_shapes=[pltpu.VMEM((2 * hd, 2 * hd), F32)],
            compiler_params=_cparams(("parallel", "parallel", "arbitrary")),
            name="rw_scan_bwd" if reverse else "rw_scan_fwd",
        )(lw, kk, kka, kt, r, v))

    return pl.pallas_call(
        _rw_out_kernel,
        out_shape=tok,
        grid=(bsz, nt),
        in_specs=[_tok_spec(d)] * 6 + [spec2, _mod_spec(d), _const_spec((1, d)), _const_spec((1, d)),
                                       _const_spec((1, d)), _const_spec((LANES, LANES)), _const_spec((d, d))],
        out_specs=_tok_spec(d),
        compiler_params=_cparams(("parallel", "parallel")),
        name="rw_out",
    )(x, ys[0], ys[1], r, v, gt, kt, mod, ln_g.reshape(1, d), ln_b.reshape(1, d), r_k.reshape(1, d), gm,
      w_o.astype(BF16))


def _axial_tables(n, ctx_len, d_rot, lead, reps):
    t = jnp.arange(n)
    row = (t // GRID_W).astype(F32)
    col = (t % GRID_W).astype(F32)
    d_axis = d_rot // 2
    inv = ROPE_BASE ** (-jnp.arange(0, d_axis, 2, dtype=F32) / d_axis)
    ang = jnp.concatenate([row[:, None] * inv, col[:, None] * inv], axis=-1)
    cos, sin = jnp.cos(ang), jnp.sin(ang)
    tail = LANES - lead - reps * d_rot
    cos_t = jnp.concatenate([jnp.ones((n, lead), F32)] + [cos, cos] * reps + [jnp.ones((n, tail), F32)], axis=1)
    sin_t = jnp.concatenate([jnp.zeros((n, lead), F32)] + [-sin, sin] * reps + [jnp.zeros((n, tail), F32)], axis=1)
    cos_t = jnp.concatenate([jnp.ones((ctx_len, LANES), F32), cos_t], axis=0)
    sin_t = jnp.concatenate([jnp.zeros((ctx_len, LANES), F32), sin_t], axis=0)
    return cos_t, sin_t


def _tab_spec():
    return pl.BlockSpec((TM, LANES), lambda b, t: (t, 0))


def _swa_proj_kernel(x_ref, g_ref, mod_ref, w_ref, gq_ref, gk_ref, gm_ref, cos_ref, sin_ref,
                     q_ref, k_ref, v_ref, *, nq, nk, scale):
    hn = _modnorm(x_ref[0], g_ref[...], mod_ref[0, 0, 0:1, :], mod_ref[0, 0, 1:2, :]).astype(BF16)
    qkv = jnp.dot(hn, w_ref[...], preferred_element_type=F32)
    cos, sin = cos_ref[...], sin_ref[...]
    inv = 1.0 / SWA_HEAD_DIM
    half = SWA_HEAD_DIM // 2

    def norm_rope(xc, gain):
        y = xc * lax.rsqrt(_group_sums(xc * xc, gm_ref[...]) * inv + NORM_EPS) * gain
        return y * cos + _rope_swap(y, half) * sin

    for c in range(nq // LANES):
        cs = slice(c * LANES, (c + 1) * LANES)
        q_ref[0, :, cs] = (norm_rope(qkv[:, cs], gq_ref[:, cs]) * scale).astype(BF16)
    for c in range(nk // LANES):
        cs = slice(c * LANES, (c + 1) * LANES)
        k_ref[0, :, cs] = norm_rope(qkv[:, nq + c * LANES:nq + (c + 1) * LANES], gk_ref[:, cs]).astype(BF16)
    v_ref[0] = qkv[:, nq + nk:].astype(BF16)


def _swa_attn_kernel(sink_ref, q_ref, k_ref, v_ref, o_ref, *, ctx_len, qb):
    t_total = k_ref.shape[1]
    j = pl.program_id(1)
    q0 = j * qb
    nloc = 3 * qb
    ks = pl.multiple_of(jnp.clip(q0 - qb, 0, t_total - nloc), qb)
    qpos = q0 + lax.broadcasted_iota(jnp.int32, (qb, nloc), 0)
    kpos = ks + lax.broadcasted_iota(jnp.int32, (qb, nloc), 1)
    ok = jnp.where(kpos >= ctx_len, jnp.where(qpos >= ctx_len, jnp.abs(qpos - kpos), SWA_WINDOW + 1), SWA_WINDOW + 1)
    ok = ok <= SWA_WINDOW
    lane = lax.broadcasted_iota(jnp.int32, (qb, LANES), 1)
    dn = (((1,), (1,)), ((), ()))
    group = SWA_Q_HEADS // SWA_KV_HEADS
    prev = None
    for h in range(SWA_Q_HEADS):
        c, a, hk = h // 2, h % 2, h // group
        q2 = q_ref[0, :, c * LANES:(c + 1) * LANES]
        qa = jnp.where((lane < LANES // 2) == (a == 0), q2, jnp.zeros_like(q2))
        kcs = slice(hk * LANES, (hk + 1) * LANES)
        kw = k_ref[0, pl.ds(ks, nloc), kcs]
        vw = v_ref[0, pl.ds(ks, nloc), kcs]
        kc = k_ref[0, 0:ctx_len, kcs]
        vc = v_ref[0, 0:ctx_len, kcs]
        s1 = jnp.where(ok, lax.dot_general(qa, kw, dn, preferred_element_type=F32), NEG_INF)
        s2 = lax.dot_general(qa, kc, dn, preferred_element_type=F32)
        snk = sink_ref[h]
        m = jnp.maximum(jnp.maximum(jnp.max(s1, axis=-1, keepdims=True), jnp.max(s2, axis=-1, keepdims=True)), snk)
        p1 = jnp.exp(s1 - m)
        p2 = jnp.exp(s2 - m)
        den = jnp.sum(p1, axis=-1, keepdims=True) + jnp.sum(p2, axis=-1, keepdims=True) + jnp.exp(snk - m)
        o = (jnp.dot(p1.astype(BF16), vw, preferred_element_type=F32)
             + jnp.dot(p2.astype(BF16), vc, preferred_element_type=F32)) / den
        if a == 0:
            prev = o
        else:
            o_ref[0, :, c * LANES:(c + 1) * LANES] = jnp.where(lane < LANES // 2, prev, o).astype(BF16)


def _layer_swa(x, mod, ctx_len, norm_g, w_qkv, q_g, k_g, sink, w_o):
    bsz, t, d = x.shape
    n = t - ctx_len
    dh = SWA_HEAD_DIM
    nq = SWA_Q_HEADS * dh
    nkv = SWA_KV_HEADS * dh

    def dup(w):
        return jnp.broadcast_to(w.reshape(d, SWA_KV_HEADS, 1, dh), (d, SWA_KV_HEADS, 2, dh)).reshape(d, 2 * nkv)

    w_ext = jnp.concatenate([w_qkv[:, :nq], dup(w_qkv[:, nq:nq + nkv]), dup(w_qkv[:, nq + nkv:])], axis=1).astype(BF16)
    cos_t, sin_t = _axial_tables(n, ctx_len, dh, 0, LANES // dh)
    gm = _block_ones([(dh, True), (dh, True)])
    q, k, v = pl.pallas_call(
        functools.partial(_swa_proj_kernel, nq=nq, nk=2 * nkv, scale=dh ** -0.5),
        out_shape=[jax.ShapeDtypeStruct((bsz, t, nq), BF16), jax.ShapeDtypeStruct((bsz, t, 2 * nkv), BF16),
                   jax.ShapeDtypeStruct((bsz, t, 2 * nkv), BF16)],
        grid=(bsz, t // TM),
        in_specs=[_tok_spec(d), _const_spec((1, d)), _mod_spec(d), _const_spec((d, nq + 4 * nkv)),
                  _const_spec((1, nq)), _const_spec((1, 2 * nkv)), _const_spec((LANES, LANES)),
                  _tab_spec(), _tab_spec()],
        out_specs=[_tok_spec(nq), _tok_spec(2 * nkv), _tok_spec(2 * nkv)],
        compiler_params=_cparams(("parallel", "parallel")),
        name="swa_proj",
    )(x, norm_g.reshape(1, d), mod, w_ext, jnp.tile(q_g, SWA_Q_HEADS).reshape(1, nq),
      jnp.tile(k_g, 2 * SWA_KV_HEADS).reshape(1, 2 * nkv), gm, cos_t, sin_t)

    qb = SWA_WINDOW
    o = pl.pallas_call(
        functools.partial(_swa_attn_kernel, ctx_len=ctx_len, qb=qb),
        out_shape=jax.ShapeDtypeStruct((bsz, t, nq), BF16),
        grid=(bsz, t // qb),
        in_specs=[pl.BlockSpec(memory_space=pltpu.SMEM),
                  pl.BlockSpec((1, qb, nq), lambda b, j: (b, j, 0)),
                  pl.BlockSpec((1, t, 2 * nkv), lambda b, j: (b, 0, 0)),
                  pl.BlockSpec((1, t, 2 * nkv), lambda b, j: (b, 0, 0))],
        out_specs=pl.BlockSpec((1, qb, nq), lambda b, j: (b, j, 0)),
        compiler_params=_cparams(("parallel", "arbitrary")),
        name="swa_attn",
    )(sink.astype(F32), q, k, v)
    return _oproj(x, o, mod, w_o.astype(BF16), 2)


def _mla_proj_kernel(x_ref, g_ref, mod_ref, wd_ref, gqn_ref, gkvn_ref, gkr_ref, wuq_ref, wukn_ref, wuv_ref,
                     gq_ref, gk_ref, gm_ref, invc_ref, cos_ref, sin_ref, q_ref, k_ref, v_ref, *, scale):
    hn = _modnorm(x_ref[0], g_ref[...], mod_ref[0, 0, 0:1, :], mod_ref[0, 0, 1:2, :]).astype(BF16)
    dd = jnp.dot(hn, wd_ref[...], preferred_element_type=F32)
    cos, sin = cos_ref[...], sin_ref[...]
    half = MLA_ROPE // 2

    def rms(z, g):
        return z * lax.rsqrt(jnp.mean(z * z, axis=-1, keepdims=True) + NORM_EPS) * g

    def rope(y):
        return y * cos + _rope_swap(y, half) * sin

    cq = rms(dd[:, :MLA_Q_RANK], gqn_ref[...]).astype(BF16)
    ckv = rms(dd[:, MLA_Q_RANK:MLA_Q_RANK + MLA_KV_RANK], gkvn_ref[...]).astype(BF16)
    kr = dd[:, MLA_Q_RANK + MLA_KV_RANK:]
    kr = kr * lax.rsqrt(jnp.sum(kr * kr, axis=-1, keepdims=True) * (1.0 / MLA_ROPE) + NORM_EPS) * gkr_ref[...]
    kr = rope(kr)
    q = jnp.dot(cq, wuq_ref[...], preferred_element_type=F32)
    kn = jnp.dot(ckv, wukn_ref[...], preferred_element_type=F32)
    invc = invc_ref[...]
    for c in range(MLA_HEADS):
        cs = slice(c * LANES, (c + 1) * LANES)
        xc = q[:, cs]
        y = xc * lax.rsqrt(_group_sums(xc * xc, gm_ref[...]) * invc + NORM_EPS) * gq_ref[...]
        q_ref[0, :, cs] = (rope(y) * scale).astype(BF16)
        xk = kn[:, cs]
        yk = xk * lax.rsqrt(_group_sums(xk * xk, gm_ref[...]) * invc + NORM_EPS) * gk_ref[...]
        k_ref[0, :, cs] = (yk + kr).astype(BF16)
    v_ref[0] = jnp.dot(ckv, wuv_ref[...], preferred_element_type=F32).astype(BF16)


def _mla_attn_kernel(q_ref, k_ref, v_ref, o_ref, *, ctx_len):
    t_total = k_ref.shape[1]
    tq = q_ref.shape[1]
    lane = lax.broadcasted_iota(jnp.int32, (tq, LANES), 1)
    dn = (((1,), (1,)), ((), ()))

    def attend(nk):
        outs = []
        for a in range(2):
            qh = q_ref[0, :, a * LANES:(a + 1) * LANES]
            kh = k_ref[0, 0:nk, a * LANES:(a + 1) * LANES]
            s = lax.dot_general(qh, kh, dn, preferred_element_type=F32)
            m = jnp.max(s, axis=-1, keepdims=True)
            p = jnp.exp(s - m)
            den = jnp.sum(p, axis=-1, keepdims=True)
            outs.append(jnp.dot(p.astype(BF16), v_ref[0, 0:nk, :], preferred_element_type=F32) / den)
        o_ref[0] = jnp.where(lane < LANES // 2, outs[0], outs[1]).astype(BF16)

    @pl.when(pl.program_id(2) == 0)
    def _():
        attend(ctx_len)

    @pl.when(pl.program_id(2) > 0)
    def _():
        attend(t_total)


def _layer_mla(x, mod, ctx_len, norm_g, w_down, q_norm_g, kv_norm_g, w_uq, w_ukv, qn_g, qr_g, kn_g, kr_g, w_o):
    bsz, t, d = x.shape
    n = t - ctx_len
    h = MLA_HEADS
    pad = LANES - MLA_NOPE - MLA_ROPE
    zc = jnp.zeros((d, MLA_NOPE), F32)
    wd_ext = jnp.concatenate([w_down[:, :MLA_Q_RANK + MLA_KV_RANK], zc, w_down[:, MLA_Q_RANK + MLA_KV_RANK:],
                              jnp.zeros((d, pad), F32)], axis=1).astype(BF16)
    wuq = jnp.pad(w_uq.reshape(MLA_Q_RANK, h, MLA_NOPE + MLA_ROPE), ((0, 0), (0, 0), (0, pad)))
    wuq = wuq.reshape(MLA_Q_RANK, h * LANES).astype(BF16)
    wukv = w_ukv.reshape(MLA_KV_RANK, h, MLA_NOPE + MLA_V)
    wukn = jnp.pad(wukv[:, :, :MLA_NOPE], ((0, 0), (0, 0), (0, LANES - MLA_NOPE))).reshape(MLA_KV_RANK, h * LANES).astype(BF16)
    wuv = wukv[:, :, MLA_NOPE:].reshape(MLA_KV_RANK, h * MLA_V).astype(BF16)
    z_r, z_p = jnp.zeros((MLA_ROPE,), F32), jnp.zeros((pad,), F32)
    gq128 = jnp.concatenate([qn_g, qr_g, z_p]).reshape(1, LANES)
    gk128 = jnp.concatenate([kn_g, z_r, z_p]).reshape(1, LANES)
    gkr128 = jnp.concatenate([jnp.zeros((MLA_NOPE,), F32), kr_g, z_p]).reshape(1, LANES)
    invc = jnp.asarray(np.concatenate([np.full(MLA_NOPE, 1.0 / MLA_NOPE), np.full(MLA_ROPE, 1.0 / MLA_ROPE),
                                       np.ones(pad)]).astype(np.float32)).reshape(1, LANES)
    gm = _block_ones([(MLA_NOPE, True), (MLA_ROPE, True), (pad, False)])
    cos_t, sin_t = _axial_tables(n, ctx_len, MLA_ROPE, MLA_NOPE, 1)
    wdn = MLA_Q_RANK + MLA_KV_RANK + LANES
    q, k, v = pl.pallas_call(
        functools.partial(_mla_proj_kernel, scale=(MLA_NOPE + MLA_ROPE) ** -0.5),
        out_shape=[jax.ShapeDtypeStruct((bsz, t, h * LANES), BF16), jax.ShapeDtypeStruct((bsz, t, h * LANES), BF16),
                   jax.ShapeDtypeStruct((bsz, t, h * MLA_V), BF16)],
        grid=(bsz, t // TM),
        in_specs=[_tok_spec(d), _const_spec((1, d)), _mod_spec(d), _const_spec((d, wdn)),
                  _const_spec((1, MLA_Q_RANK)), _const_spec((1, MLA_KV_RANK)), _const_spec((1, LANES)),
                  _const_spec((MLA_Q_RANK, h * LANES)), _const_spec((MLA_KV_RANK, h * LANES)),
                  _const_spec((MLA_KV_RANK, h * MLA_V)),
                  _const_spec((1, LANES)), _const_spec((1, LANES)), _const_spec((LANES, LANES)),
                  _const_spec((1, LANES)), _tab_spec(), _tab_spec()],
        out_specs=[_tok_spec(h * LANES), _tok_spec(h * LANES), _tok_spec(h * MLA_V)],
        compiler_params=_cparams(("parallel", "parallel")),
        name="mla_proj",
    )(x, norm_g.reshape(1, d), mod, wd_ext, q_norm_g.reshape(1, -1), kv_norm_g.reshape(1, -1), gkr128,
      wuq, wukn, wuv, gq128, gk128, gm, invc, cos_t, sin_t)

    o = pl.pallas_call(
        functools.partial(_mla_attn_kernel, ctx_len=ctx_len),
        out_shape=jax.ShapeDtypeStruct((bsz, t, h * MLA_V), BF16),
        grid=(bsz, h // 2, t // TM),
        in_specs=[pl.BlockSpec((1, TM, 2 * LANES), lambda b, g, j: (b, j, g)),
                  pl.BlockSpec((1, t, 2 * LANES), lambda b, g, j: (b, 0, g)),
                  pl.BlockSpec((1, t, LANES), lambda b, g, j: (b, 0, g))],
        out_specs=pl.BlockSpec((1, TM, LANES), lambda b, g, j: (b, j, g)),
        compiler_params=_cparams(("parallel", "parallel", "arbitrary")),
        name="mla_attn",
    )(q, k, v)
    return _oproj(x, o, mod, w_o.astype(BF16), 2)


def _moe_router_kernel(x_ref, g_ref, mod_ref, wr_ref, h_ref, lg_ref):
    h = _modnorm(x_ref[0], g_ref[...], mod_ref[0, 0, 3:4, :], mod_ref[0, 0, 4:5, :])
    h_ref[0] = h
    lg_ref[0] = lax.dot_general(wr_ref[...], h, (((1,), (1,)), ((), ())),
                                preferred_element_type=F32, precision=HIGHEST)


def _lane_cumsum_excl(x, tri):
    e, w = x.shape
    blk = tri.shape[0]
    carry = jnp.zeros((e, 1), F32)
    outs = []
    for i in range(w // blk):
        xb = x[:, i * blk:(i + 1) * blk]
        inc = jnp.dot(xb.astype(BF16), tri, preferred_element_type=F32)
        outs.append(inc - xb + carry)
        carry = carry + inc[:, blk - 1:blk]
    return outs[0] if len(outs) == 1 else jnp.concatenate(outs, axis=1)


def _moe_select_kernel(lg_ref, tri_ref, idx_ref, gate_ref, *, groups):
    lg = lg_ref[0]
    m = jnp.max(lg, axis=0, keepdims=True)
    ex = jnp.exp(lg - m)
    aff = ex / jnp.sum(ex, axis=0, keepdims=True)
    tri = tri_ref[...]
    ne = lg.shape[0]
    idx_ref[...] = jnp.zeros(idx_ref.shape, jnp.int32)
    gate_ref[...] = jnp.zeros(gate_ref.shape, F32)
    for (t0, tg, cap, s0) in groups:
        a = aff[:, t0:t0 + tg]
        bits = pltpu.bitcast(a, jnp.int32)
        thr = jnp.zeros((ne, 1), jnp.int32)
        for bit in range(30, -1, -1):
            cand = thr | (1 << bit)
            cnt = jnp.sum((bits >= cand).astype(jnp.int32), axis=1, keepdims=True)
            thr = jnp.where(cnt >= cap, cand, thr)
        gt = bits > thr
        eq = (bits == thr).astype(F32)
        need = cap - jnp.sum(gt.astype(jnp.int32), axis=1, keepdims=True)
        eq_rank = _lane_cumsum_excl(eq, tri)
        sel = jnp.where(gt, 1.0, jnp.where(eq_rank < need.astype(F32), eq, 0.0))
        pos = _lane_cumsum_excl(sel, tri)
        pos = jnp.where(sel > 0.0, pos, -1.0)
        tok = t0 + lax.broadcasted_iota(jnp.int32, (1, tg), 1)
        slot = lax.broadcasted_iota(jnp.int32, (cap, tg), 0).astype(F32)
        for e in range(ne):
            onehot = jnp.where(pos[e:e + 1, :] == slot, 1.0, 0.0).astype(BF16)
            ae = a[e:e + 1, :]
            a_hi = ae.astype(BF16)
            r1 = ae - a_hi.astype(F32)
            a_mid = r1.astype(BF16)
            a_lo = (r1 - a_mid.astype(F32)).astype(BF16)
            cols = jnp.concatenate([(tok // 64).astype(F32).astype(BF16), (tok % 64).astype(F32).astype(BF16),
                                    a_hi, a_mid, a_lo, jnp.zeros((3, tg), BF16)], axis=0)
            res = lax.dot_general(cols, onehot, (((1,), (1,)), ((), ())), preferred_element_type=F32)
            idx_ref[0, e:e + 1, s0:s0 + cap] = (res[0:1] * 64.0 + res[1:2]).astype(jnp.int32)
            gate_ref[0, e:e + 1, s0:s0 + cap] = (res[2:3] + res[3:4]) + res[4:5]


def _moe_ffn_kernel(idx_ref, h_hbm, w1_ref, w3_ref, w2_ref, y_ref, hbuf, xin, sem, *, slots):
    b = pl.program_id(0)

    @pl.when(pl.program_id(1) == 0)
    def _():
        cp = pltpu.make_async_copy(h_hbm.at[b], hbuf, sem)
        cp.start()
        cp.wait()

    def gather(c, carry):
        t = idx_ref[0, 0, 0, c]
        xin[pl.ds(c, 1), :] = hbuf[pl.ds(t, 1), :]
        return carry

    lax.fori_loop(0, slots, gather, 0)
    xb = xin[...].astype(BF16)
    h1 = jnp.dot(xb, w1_ref[0], preferred_element_type=F32)
    h3 = jnp.dot(xb, w3_ref[0], preferred_element_type=F32)
    hid = (h1 * _sigmoid(h1) * h3).astype(BF16)
    y_ref[0, 0] = jnp.dot(hid, w2_ref[0], preferred_element_type=F32)


def _moe_combine_kernel(idx_ref, gate_ref, y_ref, mod_ref, x_hbm, out_ref, sem, *, lat_slots, slots):
    b = pl.program_id(0)

    @pl.when(pl.program_id(1) == 0)
    def _():
        cp = pltpu.make_async_copy(x_hbm.at[b], out_ref.at[0], sem)
        cp.start()
        cp.wait()

    def scatter(grp):
        gvec = mod_ref[0, grp, 5:6, :]

        def body(c, carry):
            t = idx_ref[0, 0, 0, c]
            row = y_ref[0, 0, pl.ds(c, 1), :] * gate_ref[0, 0, 0, c]
            out_ref[0, pl.ds(t, 1), :] = out_ref[0, pl.ds(t, 1), :] + gvec * row
            return carry
        return body

    lax.fori_loop(0, lat_slots, scatter(1), 0)
    lax.fori_loop(lat_slots, slots, scatter(0), 0)


def _moe(x, mod, ctx_len, norm_g, w_router, w1_bf, w3_bf, w2_bf):
    bsz, t, d = x.shape
    n = t - ctx_len
    ne = w_router.shape[1]
    ff = w1_bf.shape[2]
    cap_l = max(1, EC_CAPACITY * n // ne)
    cap_c = max(1, EC_CAPACITY * ctx_len // ne)
    slots = cap_l + cap_c
    slot_pad = -(-slots // LANES) * LANES
    h, lg = pl.pallas_call(
        _moe_router_kernel,
        out_shape=[jax.ShapeDtypeStruct((bsz, t, d), F32), jax.ShapeDtypeStruct((bsz, ne, t), F32)],
        grid=(bsz, t // TM),
        in_specs=[_tok_spec(d), _const_spec((1, d)), _mod_spec(d), _const_spec((ne, d))],
        out_specs=[_tok_spec(d), pl.BlockSpec((1, ne, TM), lambda b, i: (b, 0, i))],
        compiler_params=_cparams(("parallel", "parallel")),
        name="moe_router",
    )(x, norm_g.reshape(1, d), mod, w_router.T)

    tri = jnp.asarray(np.triu(np.ones((256, 256), np.float32)), BF16)
    groups = ((ctx_len, n, cap_l, 0), (0, ctx_len, cap_c, cap_l))
    idx, gate = pl.pallas_call(
        functools.partial(_moe_select_kernel, groups=groups),
        out_shape=[jax.ShapeDtypeStruct((bsz, ne, slot_pad), jnp.int32),
                   jax.ShapeDtypeStruct((bsz, ne, slot_pad), F32)],
        grid=(bsz,),
        in_specs=[pl.BlockSpec((1, ne, t), lambda b: (b, 0, 0)), pl.BlockSpec((256, 256), lambda b: (0, 0))],
        out_specs=[pl.BlockSpec((1, ne, slot_pad), lambda b: (b, 0, 0))] * 2,
        compiler_params=_cparams(("parallel",)),
        name="moe_select",
    )(lg, tri)

    smem_spec = pl.BlockSpec((1, 1, 1, slot_pad), lambda b, e: (b, e, 0, 0), memory_space=pltpu.SMEM)
    idx = idx.reshape(bsz, ne, 1, slot_pad)
    gate = gate.reshape(bsz, ne, 1, slot_pad)
    y = pl.pallas_call(
        functools.partial(_moe_ffn_kernel, slots=slots),
        out_shape=jax.ShapeDtypeStruct((bsz, ne, slots, d), F32),
        grid=(bsz, ne),
        in_specs=[smem_spec, pl.BlockSpec(memory_space=pl.ANY),
                  pl.BlockSpec((1, d, ff), lambda b, e: (e, 0, 0)),
                  pl.BlockSpec((1, d, ff), lambda b, e: (e, 0, 0)),
                  pl.BlockSpec((1, ff, d), lambda b, e: (e, 0, 0))],
        out_specs=pl.BlockSpec((1, 1, slots, d), lambda b, e: (b, e, 0, 0)),
        scratch_shapes=[pltpu.VMEM((t, d), F32), pltpu.VMEM((slots, d), F32), pltpu.SemaphoreType.DMA],
        compiler_params=_cparams(("parallel", "arbitrary"), VMEM_BIG),
        name="moe_ffn",
    )(idx, h, w1_bf, w3_bf, w2_bf)

    return pl.pallas_call(
        functools.partial(_moe_combine_kernel, lat_slots=cap_l, slots=slots),
        out_shape=jax.ShapeDtypeStruct((bsz, t, d), F32),
        grid=(bsz, ne),
        in_specs=[smem_spec, smem_spec,
                  pl.BlockSpec((1, 1, slots, d), lambda b, e: (b, e, 0, 0)),
                  pl.BlockSpec((1, 2, 6, d), lambda b, e: (b, 0, 0, 0)),
                  pl.BlockSpec(memory_space=pl.ANY)],
        out_specs=pl.BlockSpec((1, t, d), lambda b, e: (b, 0, 0)),
        scratch_shapes=[pltpu.SemaphoreType.DMA],
        compiler_params=_cparams(("parallel", "arbitrary"), VMEM_BIG),
        name="moe_combine",
    )(idx, gate, y, mod, x)


def _mixer(i, x, mod, ctx_len, p):
    mixer, j = i % 4, i // 4
    if mixer == 0:
        return _layer_na(x, mod, ctx_len, p['norm1_g'][i], p['na_w_qkv'][j], p['na_q_g'][j], p['na_k_g'][j],
                         p['na_rpb'][j], p['na_w_o'][j])
    if mixer == 1:
        return _layer_rwkv(x, mod, ctx_len, p['norm1_g'][i], p['rw_mix'][j], p['rw_w_r'][j], p['rw_w_k'][j],
                           p['rw_w_v'][j], p['rw_w0'][j], p['rw_w1'][j], p['rw_w2'][j], p['rw_a0'][j], p['rw_a1'][j],
                           p['rw_a2'][j], p['rw_g1'][j], p['rw_g2'][j], p['rw_k_k'][j], p['rw_k_a'][j], p['rw_r_k'][j],
                           p['rw_ln_g'][j], p['rw_ln_b'][j], p['rw_w_o'][j])
    if mixer == 2:
        return _layer_mla(x, mod, ctx_len, p['norm1_g'][i], p['mla_w_down'][j], p['mla_q_norm_g'][j],
                          p['mla_kv_norm_g'][j], p['mla_w_uq'][j], p['mla_w_ukv'][j], p['mla_qn_g'][j],
                          p['mla_qr_g'][j], p['mla_kn_g'][j], p['mla_kr_g'][j], p['mla_w_o'][j])
    if mixer == 3:
        return _layer_swa(x, mod, ctx_len, p['norm1_g'][i], p['swa_w_qkv'][j], p['swa_q_g'][j], p['swa_k_g'][j],
                          p['swa_sink'][j], p['swa_w_o'][j])
    return x


_PARAM_NAMES = (
    'x', 'c', 'ctx', 'c_ctx', 'norm1_g', 'norm2_g', 'ada_w', 'ada_b',
    'na_w_qkv', 'na_q_g', 'na_k_g', 'na_rpb', 'na_w_o',
    'rw_mix', 'rw_w_r', 'rw_w_k', 'rw_w_v', 'rw_w0', 'rw_w1', 'rw_w2', 'rw_a0', 'rw_a1', 'rw_a2',
    'rw_g1', 'rw_g2', 'rw_k_k', 'rw_k_a', 'rw_r_k', 'rw_ln_g', 'rw_ln_b', 'rw_w_o',
    'mla_w_down', 'mla_q_norm_g', 'mla_kv_norm_g', 'mla_w_uq', 'mla_w_ukv',
    'mla_qn_g', 'mla_qr_g', 'mla_kn_g', 'mla_kr_g', 'mla_w_o',
    'swa_w_qkv', 'swa_q_g', 'swa_k_g', 'swa_sink', 'swa_w_o',
    'moe_router', 'moe_w1', 'moe_w3', 'moe_w2')


def kernel(*args):
    p = dict(zip(_PARAM_NAMES, args))
    ctx_len = p['ctx'].shape[1]
    assert ctx_len == TM
    depth = p['ada_w'].shape[0]
    mods = _ada_all(p['c'], p['c_ctx'], p['ada_w'], p['ada_b'])
    x = jnp.concatenate([p['ctx'], p['x']], axis=1)
    for i in range(depth):
        x = _mixer(i, x, mods[i], ctx_len, p)
        x = _moe(x, mods[i], ctx_len, p['norm2_g'][i], p['moe_router'][i], p['moe_w1'][i].astype(BF16),
                 p['moe_w3'][i].astype(BF16), p['moe_w2'][i].astype(BF16))
    return x[:, ctx_len:]
```

```python
import functools

import numpy as np
import jax
import jax.numpy as jnp
from jax import lax
from jax.experimental import pallas as pl
from jax.experimental.pallas import tpu as pltpu

F32 = jnp.float32
BF16 = jnp.bfloat16
HIGHEST = lax.Precision.HIGHEST

GRID_W = 64
ROPE_BASE = 10000.0
NORM_EPS = 1e-6
NEG_INF = -1e30
NA_HEADS = 16
NA_WIN_ROWS = 8
NA_WIN_COLS = 16
RW_HEAD = 64
RW_GN_EPS = 64e-5
MLA_HEADS = 16
MLA_Q_RANK = 384
MLA_KV_RANK = 256
MLA_NOPE = 64
MLA_ROPE = 32
MLA_V = 64
SWA_Q_HEADS = 16
SWA_KV_HEADS = 4
SWA_HEAD_DIM = 64
SWA_WINDOW = 128
N_EXPERTS = 16
EC_CAPACITY = 2

LANES = 128
TM = 256
VMEM_BIG = 56 * 1024 * 1024
VMEM_MID = 40 * 1024 * 1024


def _cparams(sem, vmem=VMEM_MID):
    return pltpu.CompilerParams(dimension_semantics=sem, vmem_limit_bytes=vmem)


def _tok_spec(width, tm=TM):
    return pl.BlockSpec((1, tm, width), lambda b, t: (b, t, 0))


def _const_spec(shape):
    nd = len(shape)
    return pl.BlockSpec(shape, lambda b, t: (0,) * nd)


def _mod_spec(d):
    return pl.BlockSpec((1, 1, 6, d), lambda b, t: (b, jnp.minimum(t, 1), 0, 0))


def _sigmoid(x):
    return 1.0 / (1.0 + jnp.exp(-x))


def _modnorm(x, g, shift, scale):
    ms = jnp.mean(x * x, axis=-1, keepdims=True)
    y = x * lax.rsqrt(ms + NORM_EPS) * g
    return y * (1.0 + scale) + shift


def _group_sums(x2, gmat):
    hi = x2.astype(BF16)
    lo = (x2 - hi.astype(F32)).astype(BF16)
    return (jnp.dot(hi, gmat, preferred_element_type=F32)
            + jnp.dot(lo, gmat, preferred_element_type=F32))


def _group_rms(x, gmat, inv_cnt, eps):
    outs = []
    for c in range(x.shape[1] // LANES):
        xc = x[:, c * LANES:(c + 1) * LANES]
        ss = _group_sums(xc * xc, gmat)
        outs.append(xc * lax.rsqrt(ss * inv_cnt + eps))
    return outs[0] if len(outs) == 1 else jnp.concatenate(outs, axis=1)


def _rope_swap(x, half):
    w = x.shape[1]
    lane = lax.broadcasted_iota(jnp.int32, x.shape, 1)
    fwd = pltpu.roll(x, w - half, 1)
    bwd = pltpu.roll(x, half, 1)
    return jnp.where((lane % (2 * half)) < half, fwd, bwd)


def _block_ones(sizes):
    m = np.zeros((LANES, LANES), np.float32)
    o = 0
    for s, on in sizes:
        if on:
            m[o:o + s, o:o + s] = 1.0
        o += s
    return jnp.asarray(m, BF16)


def _ada_kernel(cond_ref, w_ref, b_ref, o_ref):
    cnd = cond_ref[...]
    s = cnd * _sigmoid(cnd)
    o_ref[0] = jnp.dot(s, w_ref[0], preferred_element_type=F32, precision=HIGHEST) + b_ref[0]


def _ada_all(c, c_ctx, ada_w, ada_b):
    depth, d, d6 = ada_w.shape
    bsz = c.shape[0]
    rows = 16
    cond = jnp.zeros((rows, d), F32).at[:bsz].set(c).at[bsz].set(c_ctx)
    tn = 1536
    m = pl.pallas_call(
        _ada_kernel,
        out_shape=jax.ShapeDtypeStruct((depth, rows, d6), F32),
        grid=(depth, d6 // tn),
        in_specs=[pl.BlockSpec((rows, d), lambda i, j: (0, 0)),
                  pl.BlockSpec((1, d, tn), lambda i, j: (i, 0, j)),
                  pl.BlockSpec((1, 1, tn), lambda i, j: (i, 0, j))],
        out_specs=pl.BlockSpec((1, rows, tn), lambda i, j: (i, 0, j)),
        compiler_params=_cparams(("parallel", "parallel")),
        name="ada_mod",
    )(cond, ada_w, ada_b.reshape(depth, 1, d6))
    lat = m[:, :bsz].reshape(depth, bsz, 6, d)
    ctx = jnp.broadcast_to(m[:, bsz].reshape(depth, 1, 6, d), (depth, bsz, 6, d))
    return jnp.stack([ctx, lat], axis=2)


def _oproj_kernel(x_ref, o_ref, mod_ref, w_ref, out_ref, *, gate_row):
    y = jnp.dot(o_ref[0], w_ref[...], preferred_element_type=F32)
    out_ref[0] = x_ref[0] + mod_ref[0, 0, gate_row:gate_row + 1, :] * y


def _oproj(x, o, mod, w_bf16, gate_row):
    bsz, t, d = x.shape
    din = o.shape[2]
    return pl.pallas_call(
        functools.partial(_oproj_kernel, gate_row=gate_row),
        out_shape=jax.ShapeDtypeStruct(x.shape, F32),
        grid=(bsz, t // TM),
        in_specs=[_tok_spec(d), _tok_spec(din), _mod_spec(d), _const_spec((din, d))],
        out_specs=_tok_spec(d),
        compiler_params=_cparams(("parallel", "parallel")),
        name="oproj",
    )(x, o, mod, w_bf16)


def _na_proj_kernel(x_ref, g_ref, mod_ref, w_ref, gq_ref, gk_ref, gm_ref, q_ref, k_ref, v_ref, *, scale):
    d = x_ref.shape[2]
    hn = _modnorm(x_ref[0], g_ref[...], mod_ref[0, 0, 0:1, :], mod_ref[0, 0, 1:2, :]).astype(BF16)
    qkv = jnp.dot(hn, w_ref[...], preferred_element_type=F32)
    inv = 1.0 / (d // NA_HEADS)
    q = _group_rms(qkv[:, :d], gm_ref[...], inv, NORM_EPS) * (gq_ref[...] * scale)
    k = _group_rms(qkv[:, d:2 * d], gm_ref[...], inv, NORM_EPS) * gk_ref[...]
    q_ref[0] = q.astype(BF16)
    k_ref[0] = k.astype(BF16)
    v_ref[0] = qkv[:, 2 * d:].astype(BF16)


def _na_attn_kernel(q_ref, k_ref, v_ref, bias_ref, o_ref, *, ctx_len, rows, qblk_per_ctx):
    j = pl.program_id(1)
    r = jnp.maximum(j - qblk_per_ctx, 0)
    r0 = jnp.clip(r - NA_WIN_ROWS // 2, 0, rows - NA_WIN_ROWS)
    ws = pl.multiple_of(ctx_len + GRID_W * r0, GRID_W)
    nwin = NA_WIN_ROWS * GRID_W
    lane = lax.broadcasted_iota(jnp.int32, (GRID_W, LANES), 1)
    dn = (((1,), (1,)), ((), ()))
    for g in range(q_ref.shape[2] // LANES):
        cs = slice(g * LANES, (g + 1) * LANES)
        q2 = q_ref[0, :, cs]
        kw = k_ref[0, pl.ds(ws, nwin), cs]
        vw = v_ref[0, pl.ds(ws, nwin), cs]
        kc = k_ref[0, 0:ctx_len, cs]
        vc = v_ref[0, 0:ctx_len, cs]
        outs = []
        for a in range(2):
            qa = jnp.where((lane < LANES // 2) == (a == 0), q2, jnp.zeros_like(q2))
            s1 = lax.dot_general(qa, kw, dn, preferred_element_type=F32) + bias_ref[0, 2 * g + a]
            s2 = lax.dot_general(qa, kc, dn, preferred_element_type=F32)
            m = jnp.maximum(jnp.max(s1, axis=-1, keepdims=True), jnp.max(s2, axis=-1, keepdims=True))
            p1 = jnp.exp(s1 - m)
            p2 = jnp.exp(s2 - m)
            den = jnp.sum(p1, axis=-1, keepdims=True) + jnp.sum(p2, axis=-1, keepdims=True)
            o = (jnp.dot(p1.astype(BF16), vw, preferred_element_type=F32)
                 + jnp.dot(p2.astype(BF16), vc, preferred_element_type=F32))
            outs.append(o / den)
        o_ref[0, :, cs] = jnp.where(lane < LANES // 2, outs[0], outs[1]).astype(BF16)


def _na_bias_table(rpb):
    h = rpb.shape[0]
    col = np.arange(GRID_W)
    cstart = np.clip(col - NA_WIN_COLS // 2, 0, GRID_W - NA_WIN_COLS)
    ok = (col[None, :] >= cstart[:, None]) & (col[None, :] < cstart[:, None] + NA_WIN_COLS)
    padc = GRID_W - NA_WIN_COLS
    rp = jnp.pad(rpb.astype(F32), ((0, 0), (0, 0), (padc, padc)))
    by_c = jnp.stack([rp[:, :, GRID_W - 1 - c:2 * GRID_W - 1 - c] for c in range(GRID_W)], axis=1)
    t = jnp.stack([by_c[:, :, NA_WIN_ROWS - 1 - dl:2 * NA_WIN_ROWS - 1 - dl] for dl in range(NA_WIN_ROWS)],
                  axis=0)
    t = jnp.where(jnp.asarray(ok)[None, None, :, None, :], t, NEG_INF)
    t = t.reshape(NA_WIN_ROWS, h, GRID_W, NA_WIN_ROWS * GRID_W)
    return jnp.concatenate([t, jnp.full((1,) + t.shape[1:], NEG_INF, F32)], axis=0)


def _layer_na(x, mod, ctx_len, norm_g, w_qkv, q_g, k_g, rpb, w_o):
    bsz, t, d = x.shape
    n = t - ctx_len
    rows = n // GRID_W
    dh = d // NA_HEADS
    gm = _block_ones([(dh, True), (dh, True)])
    q, k, v = pl.pallas_call(
        functools.partial(_na_proj_kernel, scale=dh ** -0.5),
        out_shape=[jax.ShapeDtypeStruct((bsz, t, d), BF16)] * 3,
        grid=(bsz, t // TM),
        in_specs=[_tok_spec(d), _const_spec((1, d)), _mod_spec(d), _const_spec((d, 3 * d)),
                  _const_spec((1, d)), _const_spec((1, d)), _const_spec((LANES, LANES))],
        out_specs=[_tok_spec(d)] * 3,
        compiler_params=_cparams(("parallel", "parallel")),
        name="na_proj",
    )(x, norm_g.reshape(1, d), mod, w_qkv.astype(BF16),
      jnp.tile(q_g, NA_HEADS).reshape(1, d), jnp.tile(k_g, NA_HEADS).reshape(1, d), gm)

    table = _na_bias_table(rpb)
    qpc = ctx_len // GRID_W

    def bias_idx(b, j):
        r = jnp.maximum(j - qpc, 0)
        delta = r - jnp.clip(r - NA_WIN_ROWS // 2, 0, rows - NA_WIN_ROWS)
        return (jnp.where(j < qpc, NA_WIN_ROWS, delta), 0, 0, 0)

    o = pl.pallas_call(
        functools.partial(_na_attn_kernel, ctx_len=ctx_len, rows=rows, qblk_per_ctx=qpc),
        out_shape=jax.ShapeDtypeStruct((bsz, t, d), BF16),
        grid=(bsz, t // GRID_W),
        in_specs=[pl.BlockSpec((1, GRID_W, d), lambda b, j: (b, j, 0)),
                  pl.BlockSpec((1, t, d), lambda b, j: (b, 0, 0)),
                  pl.BlockSpec((1, t, d), lambda b, j: (b, 0, 0)),
                  pl.BlockSpec((1, NA_HEADS, GRID_W, NA_WIN_ROWS * GRID_W), bias_idx)],
        out_specs=pl.BlockSpec((1, GRID_W, d), lambda b, j: (b, j, 0)),
        compiler_params=_cparams(("parallel", "arbitrary"), VMEM_BIG),
        name="na_attn",
    )(q, k, v, table)
    return _oproj(x, o, mod, w_o.astype(BF16), 2)


RW_PAIRS = 8
RW_CHUNK = 64


def _rw_proj_kernel(x_ref, xp_ref, xn_ref, g_ref, mod_ref, mix_ref, wr_ref, wk_ref, wv_ref, g1_ref, g2_ref,
                    w1_ref, w2_ref, w0_ref, a1_ref, a2_ref, a0_ref, kk_g_ref, ka_ref, gm_ref,
                    r_ref, v_ref, kk_ref, gt_ref, lw_ref, kka_ref, kt_ref):
    t = pl.program_id(1)
    nt = pl.num_programs(1)
    g = g_ref[...]
    shift, scale = mod_ref[0, 0, 0:1, :], mod_ref[0, 0, 1:2, :]
    h = _modnorm(x_ref[0], g, shift, scale)
    tm = h.shape[0]
    hp8 = _modnorm(xp_ref[0], g, shift, scale)
    hn8 = _modnorm(xn_ref[0], g, shift, scale)
    prev_row = jnp.where(t >= 2, hp8[7:8, :], 0.0)
    next_row = jnp.where(jnp.logical_and(t >= 1, t < nt - 1), hn8[0:1, :], 0.0)
    row = lax.broadcasted_iota(jnp.int32, h.shape, 0)
    h_prev = jnp.where(row == 0, prev_row, pltpu.roll(h, 1, 0))
    h_next = jnp.where(row == tm - 1, next_row, pltpu.roll(h, tm - 1, 0))
    xx = 0.5 * (h_prev + h_next) - h

    def mixed(s):
        return (h + xx * mix_ref[s:s + 1, :]).astype(BF16)

    r = jnp.dot(mixed(0), wr_ref[...], preferred_element_type=F32)
    k = jnp.dot(mixed(2), wk_ref[...], preferred_element_type=F32)
    v = jnp.dot(mixed(3), wv_ref[...], preferred_element_type=F32)
    gl = _sigmoid(jnp.dot(mixed(5), g1_ref[...], preferred_element_type=F32)).astype(BF16)
    gt_ref[0] = jnp.dot(gl, g2_ref[...], preferred_element_type=F32)
    dlo = jnp.tanh(jnp.dot(mixed(1), w1_ref[...], preferred_element_type=F32)).astype(BF16)
    alo = jnp.dot(mixed(4), a1_ref[...], preferred_element_type=F32).astype(BF16)
    kk = k * kk_g_ref[...]
    nrm = jnp.concatenate([jnp.sqrt(_group_sums(kk[:, c * LANES:(c + 1) * LANES] ** 2, gm_ref[...]))
                           for c in range(kk.shape[1] // LANES)], axis=1)
    kk = kk / jnp.maximum(nrm, 1e-12)
    r_ref[0] = r
    v_ref[0] = v
    kk_ref[0] = kk
    for z in range(2):
        dec = w0_ref[z:z + 1, :] + jnp.dot(dlo, w2_ref[z], preferred_element_type=F32)
        sp = jnp.maximum(-dec, 0.0) + jnp.log(1.0 + jnp.exp(-jnp.abs(dec)))
        lw_ref[z, 0] = -jnp.exp(-sp - 0.5)
        a = _sigmoid(a0_ref[z:z + 1, :] + jnp.dot(alo, a2_ref[z], preferred_element_type=F32))
        kka_ref[z, 0] = kk * a
        kt_ref[z, 0] = k * (1.0 + (a - 1.0) * ka_ref[...])


def _rw_scan_kernel(lw_ref, kk_ref, kka_ref, kt_ref, r_ref, v_ref, y_ref, s_ref, *, reverse, pairs):
    c = RW_CHUNK
    hd = RW_HEAD
    pw = 2 * hd

    @pl.when(pl.program_id(2) == 0)
    def _():
        s_ref[...] = jnp.zeros(s_ref.shape, F32)

    def split(a):
        hi = a.astype(BF16)
        return hi, (a - hi.astype(F32)).astype(BF16)

    def dot3(a, b, dn):
        return (lax.dot_general(a[0], b[0], dn, preferred_element_type=F32)
                + lax.dot_general(a[0], b[1], dn, preferred_element_type=F32)
                + lax.dot_general(a[1], b[0], dn, preferred_element_type=F32))

    def mm(a, b):
        return dot3(a, b, (((1,), (0,)), ((), ())))

    def mm_nt(a, b):
        return dot3(a, b, (((1,), (1,)), ((), ())))

    rr = lax.broadcasted_iota(jnp.int32, (c, c), 0)
    cc = lax.broadcasted_iota(jnp.int32, (c, c), 1)
    before = (cc > rr) if reverse else (cc < rr)
    strict = before.astype(F32)
    incl = jnp.logical_or(before, cc == rr).astype(F32)
    eye = (cc == rr).astype(F32)

    lw = lw_ref[0, 0]
    cl = mm(split(incl), split(lw))
    e_in = jnp.exp(cl)
    e_inv = jnp.exp(-cl)
    pt_all = kk_ref[0] * jnp.exp(cl - lw)
    rt_all = r_ref[0] * e_in
    qt_all = kka_ref[0, 0] * e_inv
    kt_all = kt_ref[0, 0] * e_inv
    v_all = v_ref[0]
    last = 0 if reverse else c - 1

    lane = lax.broadcasted_iota(jnp.int32, (c, pw), 1)
    ri = lax.broadcasted_iota(jnp.int32, (pw, pw), 0)
    ci = lax.broadcasted_iota(jnp.int32, (pw, pw), 1)
    same = (ri < hd) == (ci < hd)
    gps = range(pairs)
    heads = [(gp, a) for gp in gps for a in range(2)]
    lss = [slice(gp * pw, (gp + 1) * pw) for gp in gps]
    qt = [split(qt_all[:, ls]) for ls in lss]
    kt = [split(kt_all[:, ls]) for ls in lss]
    s2f = [s_ref[gp] for gp in gps]
    s2 = [split(x) for x in s2f]

    def own(a):
        return (lane >= hd) if a else (lane < hd)

    pa = [split(jnp.where(own(a), pt_all[:, lss[gp]], 0.0)) for gp, a in heads]
    ra = [split(jnp.where(own(a), rt_all[:, lss[gp]], 0.0)) for gp, a in heads]
    va = [split(jnp.where(own(a), v_all[:, lss[gp]], 0.0)) for gp, a in heads]
    npow = [-(mm_nt(pa[h], qt[gp]) * strict) for h, (gp, a) in enumerate(heads)]
    inv = [eye + x for x in npow]
    for _ in range(5):
        nps = [split(x) for x in npow]
        npow = [mm(x, x) for x in nps]
        inv = [i + mm(split(i), split(x)) for i, x in zip(inv, npow)]
    a_pk = [split(mm_nt(pa[h], kt[gp]) * strict) for h, (gp, a) in enumerate(heads)]
    rhs = [mm_nt(pa[h], s2[gp]) + mm(a_pk[h], va[h]) for h, (gp, a) in enumerate(heads)]
    ua = [mm(split(i), split(x)) for i, x in zip(inv, rhs)]
    a_rq = [split(mm_nt(ra[h], qt[gp]) * incl) for h, (gp, a) in enumerate(heads)]
    a_rk = [split(mm_nt(ra[h], kt[gp]) * incl) for h, (gp, a) in enumerate(heads)]
    ya = [mm_nt(ra[h], s2[gp]) - mm(a_rq[h], split(ua[h])) + mm(a_rk[h], va[h])
          for h, (gp, a) in enumerate(heads)]
    for gp in gps:
        ls = lss[gp]
        y_ref[0, :, ls] = ya[2 * gp] + ya[2 * gp + 1]
        u2 = ua[2 * gp] + ua[2 * gp + 1]
        g_last = e_in[last:last + 1, ls]
        upd = s2f[gp] - mm(split(u2.T), qt[gp]) + mm(split(v_all[:, ls].T), kt[gp])
        s_ref[gp] = jnp.where(same, upd, 0.0) * g_last


def _rw_out_kernel(x_ref, yf_ref, yb_ref, r_ref, v_ref, gt_ref, kt_ref, mod_ref, lng_ref, lnb_ref, rk_ref,
                   gm_ref, wo_ref, out_ref):
    y = yf_ref[0] + yb_ref[0]
    r = r_ref[0]
    brk = (kt_ref[0, 0] + kt_ref[1, 0]) * r * rk_ref[...]
    inv = 1.0 / RW_HEAD
    cols = []
    for c in range(y.shape[1] // LANES):
        cs = slice(c * LANES, (c + 1) * LANES)
        yc = y[:, cs]
        mu = _group_sums(yc, gm_ref[...]) * inv
        dv = yc - mu
        var = _group_sums(dv * dv, gm_ref[...]) * inv
        yn = dv * lax.rsqrt(var + RW_GN_EPS) * lng_ref[:, cs] + lnb_ref[:, cs]
        bonus = _group_sums(brk[:, cs], gm_ref[...]) * v_ref[0, :, cs]
        cols.append(yn + bonus)
    o = (jnp.concatenate(cols, axis=1) * gt_ref[0]).astype(BF16)
    out_ref[0] = x_ref[0] + mod_ref[0, 0, 2:3, :] * jnp.dot(o, wo_ref[...], preferred_element_type=F32)


def _layer_rwkv(x, mod, ctx_len, norm_g, mix, w_r, w_k, w_v, w0, w1, w2, a0, a1, a2, g1, g2,
                k_k, k_a, r_k, ln_g, ln_b, w_o):
    bsz, t, d = x.shape
    hd = RW_HEAD
    lora_d, lora_a, lora_g = w1.shape[2], a1.shape[2], g1.shape[1]

    def cat_in(w):
        return jnp.concatenate([w[0], w[1]], axis=1).astype(BF16)

    def pad_out(w):
        z = jnp.zeros_like(w[0])
        return jnp.stack([jnp.concatenate([w[0], z], axis=0), jnp.concatenate([z, w[1]], axis=0)]).astype(BF16)

    gm = _block_ones([(hd, True), (hd, True)])
    nt = t // TM
    sub = TM // 8
    tok = jax.ShapeDtypeStruct((bsz, t, d), F32)
    tok2 = jax.ShapeDtypeStruct((2, bsz, t, d), F32)
    spec2 = pl.BlockSpec((2, 1, TM, d), lambda b, i: (0, b, i, 0))
    r, v, kk, gt, lw, kka, kt = pl.pallas_call(
        _rw_proj_kernel,
        out_shape=[tok, tok, tok, tok, tok2, tok2, tok2],
        grid=(bsz, nt),
        in_specs=[_tok_spec(d),
                  pl.BlockSpec((1, 8, d), lambda b, i: (b, jnp.maximum(i * sub - 1, 0), 0)),
                  pl.BlockSpec((1, 8, d), lambda b, i: (b, jnp.minimum((i + 1) * sub, nt * sub - 1), 0)),
                  _const_spec((1, d)), _mod_spec(d), _const_spec((6, d)),
                  _const_spec((d, d)), _const_spec((d, d)), _const_spec((d, d)),
                  _const_spec((d, lora_g)), _const_spec((lora_g, d)),
                  _const_spec((d, 2 * lora_d)), _const_spec((2, 2 * lora_d, d)), _const_spec((2, d)),
                  _const_spec((d, 2 * lora_a)), _const_spec((2, 2 * lora_a, d)), _const_spec((2, d)),
                  _const_spec((1, d)), _const_spec((1, d)), _const_spec((LANES, LANES))],
        out_specs=[_tok_spec(d)] * 4 + [spec2] * 3,
        compiler_params=_cparams(("parallel", "parallel"), VMEM_BIG),
        name="rw_proj",
    )(x, x, x, norm_g.reshape(1, d), mod, mix, w_r.astype(BF16), w_k.astype(BF16), w_v.astype(BF16),
      g1.astype(BF16), g2.astype(BF16), cat_in(w1), pad_out(w2), w0, cat_in(a1), pad_out(a2), a0,
      k_k.reshape(1, d), k_a.reshape(1, d), gm)

    nc = t // RW_CHUNK
    ncc = ctx_len // RW_CHUNK
    pairs = RW_PAIRS
    def scan_specs(z, reverse):
        def chunk(s):
            return jnp.where(s < ncc, ncc - 1 - s, nc - 1 - s + ncc) if reverse else s

        def dir_map(b, g, s):
            return (z, b, chunk(s), g)

        def sh_map(b, g, s):
            return (b, chunk(s), g)

        return (pl.BlockSpec((1, 1, RW_CHUNK, 2 * hd * pairs), dir_map),
                pl.BlockSpec((1, RW_CHUNK, 2 * hd * pairs), sh_map))

    ys = []
    for z, reverse in enumerate((False, True)):
        dir_spec, sh_spec = scan_specs(z, reverse)
        ys.append(pl.pallas_call(
            functools.partial(_rw_scan_kernel, reverse=reverse, pairs=pairs),
            out_shape=tok,
            grid=(bsz, d // (2 * hd * pairs), nc),
            in_specs=[dir_spec, sh_spec, dir_spec, dir_spec, sh_spec, sh_spec],
            out_specs=sh_spec,
            scratch_shapes=[pltpu.VMEM((pairs, 2 * hd, 2 * hd), F32)],
            compiler_params=_cparams(("parallel", "parallel", "arbitrary")),
            name="rw_scan_bwd" if reverse else "rw_scan_fwd",
        )(lw, kk, kka, kt, r, v))

    return pl.pallas_call(
        _rw_out_kernel,
        out_shape=tok,
        grid=(bsz, nt),
        in_specs=[_tok_spec(d)] * 6 + [spec2, _mod_spec(d), _const_spec((1, d)), _const_spec((1, d)),
                                       _const_spec((1, d)), _const_spec((LANES, LANES)), _const_spec((d, d))],
        out_specs=_tok_spec(d),
        compiler_params=_cparams(("parallel", "parallel")),
        name="rw_out",
    )(x, ys[0], ys[1], r, v, gt, kt, mod, ln_g.reshape(1, d), ln_b.reshape(1, d), r_k.reshape(1, d), gm,
      w_o.astype(BF16))


def _axial_tables(n, ctx_len, d_rot, lead, reps):
    t = jnp.arange(n)
    row = (t // GRID_W).astype(F32)
    col = (t % GRID_W).astype(F32)
    d_axis = d_rot // 2
    inv = ROPE_BASE ** (-jnp.arange(0, d_axis, 2, dtype=F32) / d_axis)
    ang = jnp.concatenate([row[:, None] * inv, col[:, None] * inv], axis=-1)
    cos, sin = jnp.cos(ang), jnp.sin(ang)
    tail = LANES - lead - reps * d_rot
    cos_t = jnp.concatenate([jnp.ones((n, lead), F32)] + [cos, cos] * reps + [jnp.ones((n, tail), F32)], axis=1)
    sin_t = jnp.concatenate([jnp.zeros((n, lead), F32)] + [-sin, sin] * reps + [jnp.zeros((n, tail), F32)], axis=1)
    cos_t = jnp.concatenate([jnp.ones((ctx_len, LANES), F32), cos_t], axis=0)
    sin_t = jnp.concatenate([jnp.zeros((ctx_len, LANES), F32), sin_t], axis=0)
    return cos_t, sin_t


def _tab_spec():
    return pl.BlockSpec((TM, LANES), lambda b, t: (t, 0))


def _swa_proj_kernel(x_ref, g_ref, mod_ref, w_ref, gq_ref, gk_ref, gm_ref, cos_ref, sin_ref,
                     q_ref, k_ref, v_ref, *, nq, nk, scale):
    hn = _modnorm(x_ref[0], g_ref[...], mod_ref[0, 0, 0:1, :], mod_ref[0, 0, 1:2, :]).astype(BF16)
    qkv = jnp.dot(hn, w_ref[...], preferred_element_type=F32)
    cos, sin = cos_ref[...], sin_ref[...]
    inv = 1.0 / SWA_HEAD_DIM
    half = SWA_HEAD_DIM // 2

    def norm_rope(xc, gain):
        y = xc * lax.rsqrt(_group_sums(xc * xc, gm_ref[...]) * inv + NORM_EPS) * gain
        return y * cos + _rope_swap(y, half) * sin

    for c in range(nq // LANES):
        cs = slice(c * LANES, (c + 1) * LANES)
        q_ref[0, :, cs] = (norm_rope(qkv[:, cs], gq_ref[:, cs]) * scale).astype(BF16)
    for c in range(nk // LANES):
        cs = slice(c * LANES, (c + 1) * LANES)
        k_ref[0, :, cs] = norm_rope(qkv[:, nq + c * LANES:nq + (c + 1) * LANES], gk_ref[:, cs]).astype(BF16)
    v_ref[0] = qkv[:, nq + nk:].astype(BF16)


def _swa_attn_kernel(sink_ref, q_ref, k_ref, v_ref, o_ref, *, ctx_len, qb):
    t_total = k_ref.shape[1]
    j = pl.program_id(1)
    q0 = j * qb
    nloc = 3 * qb
    ks = pl.multiple_of(jnp.clip(q0 - qb, 0, t_total - nloc), qb)
    qpos = q0 + lax.broadcasted_iota(jnp.int32, (qb, nloc), 0)
    kpos = ks + lax.broadcasted_iota(jnp.int32, (qb, nloc), 1)
    ok = jnp.where(kpos >= ctx_len, jnp.where(qpos >= ctx_len, jnp.abs(qpos - kpos), SWA_WINDOW + 1), SWA_WINDOW + 1)
    ok = ok <= SWA_WINDOW
    lane = lax.broadcasted_iota(jnp.int32, (qb, LANES), 1)
    dn = (((1,), (1,)), ((), ()))
    group = SWA_Q_HEADS // SWA_KV_HEADS
    prev = None
    for h in range(SWA_Q_HEADS):
        c, a, hk = h // 2, h % 2, h // group
        q2 = q_ref[0, :, c * LANES:(c + 1) * LANES]
        qa = jnp.where((lane < LANES // 2) == (a == 0), q2, jnp.zeros_like(q2))
        kcs = slice(hk * LANES, (hk + 1) * LANES)
        kw = k_ref[0, pl.ds(ks, nloc), kcs]
        vw = v_ref[0, pl.ds(ks, nloc), kcs]
        kc = k_ref[0, 0:ctx_len, kcs]
        vc = v_ref[0, 0:ctx_len, kcs]
        s1 = jnp.where(ok, lax.dot_general(qa, kw, dn, preferred_element_type=F32), NEG_INF)
        s2 = lax.dot_general(qa, kc, dn, preferred_element_type=F32)
        snk = sink_ref[h]
        m = jnp.maximum(jnp.maximum(jnp.max(s1, axis=-1, keepdims=True), jnp.max(s2, axis=-1, keepdims=True)), snk)
        p1 = jnp.exp(s1 - m)
        p2 = jnp.exp(s2 - m)
        den = jnp.sum(p1, axis=-1, keepdims=True) + jnp.sum(p2, axis=-1, keepdims=True) + jnp.exp(snk - m)
        o = (jnp.dot(p1.astype(BF16), vw, preferred_element_type=F32)
             + jnp.dot(p2.astype(BF16), vc, preferred_element_type=F32)) / den
        if a == 0:
            prev = o
        else:
            o_ref[0, :, c * LANES:(c + 1) * LANES] = jnp.where(lane < LANES // 2, prev, o).astype(BF16)


def _layer_swa(x, mod, ctx_len, norm_g, w_qkv, q_g, k_g, sink, w_o):
    bsz, t, d = x.shape
    n = t - ctx_len
    dh = SWA_HEAD_DIM
    nq = SWA_Q_HEADS * dh
    nkv = SWA_KV_HEADS * dh

    def dup(w):
        return jnp.broadcast_to(w.reshape(d, SWA_KV_HEADS, 1, dh), (d, SWA_KV_HEADS, 2, dh)).reshape(d, 2 * nkv)

    w_ext = jnp.concatenate([w_qkv[:, :nq], dup(w_qkv[:, nq:nq + nkv]), dup(w_qkv[:, nq + nkv:])], axis=1).astype(BF16)
    cos_t, sin_t = _axial_tables(n, ctx_len, dh, 0, LANES // dh)
    gm = _block_ones([(dh, True), (dh, True)])
    q, k, v = pl.pallas_call(
        functools.partial(_swa_proj_kernel, nq=nq, nk=2 * nkv, scale=dh ** -0.5),
        out_shape=[jax.ShapeDtypeStruct((bsz, t, nq), BF16), jax.ShapeDtypeStruct((bsz, t, 2 * nkv), BF16),
                   jax.ShapeDtypeStruct((bsz, t, 2 * nkv), BF16)],
        grid=(bsz, t // TM),
        in_specs=[_tok_spec(d), _const_spec((1, d)), _mod_spec(d), _const_spec((d, nq + 4 * nkv)),
                  _const_spec((1, nq)), _const_spec((1, 2 * nkv)), _const_spec((LANES, LANES)),
                  _tab_spec(), _tab_spec()],
        out_specs=[_tok_spec(nq), _tok_spec(2 * nkv), _tok_spec(2 * nkv)],
        compiler_params=_cparams(("parallel", "parallel")),
        name="swa_proj",
    )(x, norm_g.reshape(1, d), mod, w_ext, jnp.tile(q_g, SWA_Q_HEADS).reshape(1, nq),
      jnp.tile(k_g, 2 * SWA_KV_HEADS).reshape(1, 2 * nkv), gm, cos_t, sin_t)

    qb = SWA_WINDOW
    o = pl.pallas_call(
        functools.partial(_swa_attn_kernel, ctx_len=ctx_len, qb=qb),
        out_shape=jax.ShapeDtypeStruct((bsz, t, nq), BF16),
        grid=(bsz, t // qb),
        in_specs=[pl.BlockSpec(memory_space=pltpu.SMEM),
                  pl.BlockSpec((1, qb, nq), lambda b, j: (b, j, 0)),
                  pl.BlockSpec((1, t, 2 * nkv), lambda b, j: (b, 0, 0)),
                  pl.BlockSpec((1, t, 2 * nkv), lambda b, j: (b, 0, 0))],
        out_specs=pl.BlockSpec((1, qb, nq), lambda b, j: (b, j, 0)),
        compiler_params=_cparams(("parallel", "arbitrary")),
        name="swa_attn",
    )(sink.astype(F32), q, k, v)
    return _oproj(x, o, mod, w_o.astype(BF16), 2)


def _mla_proj_kernel(x_ref, g_ref, mod_ref, wd_ref, gqn_ref, gkvn_ref, gkr_ref, wuq_ref, wukn_ref, wuv_ref,
                     gq_ref, gk_ref, gm_ref, invc_ref, cos_ref, sin_ref, q_ref, k_ref, v_ref, *, scale):
    hn = _modnorm(x_ref[0], g_ref[...], mod_ref[0, 0, 0:1, :], mod_ref[0, 0, 1:2, :]).astype(BF16)
    dd = jnp.dot(hn, wd_ref[...], preferred_element_type=F32)
    cos, sin = cos_ref[...], sin_ref[...]
    half = MLA_ROPE // 2

    def rms(z, g):
        return z * lax.rsqrt(jnp.mean(z * z, axis=-1, keepdims=True) + NORM_EPS) * g

    def rope(y):
        return y * cos + _rope_swap(y, half) * sin

    cq = rms(dd[:, :MLA_Q_RANK], gqn_ref[...]).astype(BF16)
    ckv = rms(dd[:, MLA_Q_RANK:MLA_Q_RANK + MLA_KV_RANK], gkvn_ref[...]).astype(BF16)
    kr = dd[:, MLA_Q_RANK + MLA_KV_RANK:]
    kr = kr * lax.rsqrt(jnp.sum(kr * kr, axis=-1, keepdims=True) * (1.0 / MLA_ROPE) + NORM_EPS) * gkr_ref[...]
    kr = rope(kr)
    q = jnp.dot(cq, wuq_ref[...], preferred_element_type=F32)
    kn = jnp.dot(ckv, wukn_ref[...], preferred_element_type=F32)
    invc = invc_ref[...]
    for c in range(MLA_HEADS):
        cs = slice(c * LANES, (c + 1) * LANES)
        xc = q[:, cs]
        y = xc * lax.rsqrt(_group_sums(xc * xc, gm_ref[...]) * invc + NORM_EPS) * gq_ref[...]
        q_ref[0, :, cs] = (rope(y) * scale).astype(BF16)
        xk = kn[:, cs]
        yk = xk * lax.rsqrt(_group_sums(xk * xk, gm_ref[...]) * invc + NORM_EPS) * gk_ref[...]
        k_ref[0, :, cs] = (yk + kr).astype(BF16)
    v_ref[0] = jnp.dot(ckv, wuv_ref[...], preferred_element_type=F32).astype(BF16)


def _mla_attn_kernel(q_ref, k_ref, v_ref, o_ref, *, ctx_len):
    t_total = k_ref.shape[1]
    tq = q_ref.shape[1]
    lane = lax.broadcasted_iota(jnp.int32, (tq, LANES), 1)
    dn = (((1,), (1,)), ((), ()))

    def attend(nk):
        outs = []
        for a in range(2):
            qh = q_ref[0, :, a * LANES:(a + 1) * LANES]
            kh = k_ref[0, 0:nk, a * LANES:(a + 1) * LANES]
            s = lax.dot_general(qh, kh, dn, preferred_element_type=F32)
            m = jnp.max(s, axis=-1, keepdims=True)
            p = jnp.exp(s - m)
            den = jnp.sum(p, axis=-1, keepdims=True)
            outs.append(jnp.dot(p.astype(BF16), v_ref[0, 0:nk, :], preferred_element_type=F32) / den)
        o_ref[0] = jnp.where(lane < LANES // 2, outs[0], outs[1]).astype(BF16)

    @pl.when(pl.program_id(2) == 0)
    def _():
        attend(ctx_len)

    @pl.when(pl.program_id(2) > 0)
    def _():
        attend(t_total)


def _layer_mla(x, mod, ctx_len, norm_g, w_down, q_norm_g, kv_norm_g, w_uq, w_ukv, qn_g, qr_g, kn_g, kr_g, w_o):
    bsz, t, d = x.shape
    n = t - ctx_len
    h = MLA_HEADS
    pad = LANES - MLA_NOPE - MLA_ROPE
    zc = jnp.zeros((d, MLA_NOPE), F32)
    wd_ext = jnp.concatenate([w_down[:, :MLA_Q_RANK + MLA_KV_RANK], zc, w_down[:, MLA_Q_RANK + MLA_KV_RANK:],
                              jnp.zeros((d, pad), F32)], axis=1).astype(BF16)
    wuq = jnp.pad(w_uq.reshape(MLA_Q_RANK, h, MLA_NOPE + MLA_ROPE), ((0, 0), (0, 0), (0, pad)))
    wuq = wuq.reshape(MLA_Q_RANK, h * LANES).astype(BF16)
    wukv = w_ukv.reshape(MLA_KV_RANK, h, MLA_NOPE + MLA_V)
    wukn = jnp.pad(wukv[:, :, :MLA_NOPE], ((0, 0), (0, 0), (0, LANES - MLA_NOPE))).reshape(MLA_KV_RANK, h * LANES).astype(BF16)
    wuv = wukv[:, :, MLA_NOPE:].reshape(MLA_KV_RANK, h * MLA_V).astype(BF16)
    z_r, z_p = jnp.zeros((MLA_ROPE,), F32), jnp.zeros((pad,), F32)
    gq128 = jnp.concatenate([qn_g, qr_g, z_p]).reshape(1, LANES)
    gk128 = jnp.concatenate([kn_g, z_r, z_p]).reshape(1, LANES)
    gkr128 = jnp.concatenate([jnp.zeros((MLA_NOPE,), F32), kr_g, z_p]).reshape(1, LANES)
    invc = jnp.asarray(np.concatenate([np.full(MLA_NOPE, 1.0 / MLA_NOPE), np.full(MLA_ROPE, 1.0 / MLA_ROPE),
                                       np.ones(pad)]).astype(np.float32)).reshape(1, LANES)
    gm = _block_ones([(MLA_NOPE, True), (MLA_ROPE, True), (pad, False)])
    cos_t, sin_t = _axial_tables(n, ctx_len, MLA_ROPE, MLA_NOPE, 1)
    wdn = MLA_Q_RANK + MLA_KV_RANK + LANES
    q, k, v = pl.pallas_call(
        functools.partial(_mla_proj_kernel, scale=(MLA_NOPE + MLA_ROPE) ** -0.5),
        out_shape=[jax.ShapeDtypeStruct((bsz, t, h * LANES), BF16), jax.ShapeDtypeStruct((bsz, t, h * LANES), BF16),
                   jax.ShapeDtypeStruct((bsz, t, h * MLA_V), BF16)],
        grid=(bsz, t // TM),
        in_specs=[_tok_spec(d), _const_spec((1, d)), _mod_spec(d), _const_spec((d, wdn)),
                  _const_spec((1, MLA_Q_RANK)), _const_spec((1, MLA_KV_RANK)), _const_spec((1, LANES)),
                  _const_spec((MLA_Q_RANK, h * LANES)), _const_spec((MLA_KV_RANK, h * LANES)),
                  _const_spec((MLA_KV_RANK, h * MLA_V)),
                  _const_spec((1, LANES)), _const_spec((1, LANES)), _const_spec((LANES, LANES)),
                  _const_spec((1, LANES)), _tab_spec(), _tab_spec()],
        out_specs=[_tok_spec(h * LANES), _tok_spec(h * LANES), _tok_spec(h * MLA_V)],
        compiler_params=_cparams(("parallel", "parallel")),
        name="mla_proj",
    )(x, norm_g.reshape(1, d), mod, wd_ext, q_norm_g.reshape(1, -1), kv_norm_g.reshape(1, -1), gkr128,
      wuq, wukn, wuv, gq128, gk128, gm, invc, cos_t, sin_t)

    o = pl.pallas_call(
        functools.partial(_mla_attn_kernel, ctx_len=ctx_len),
        out_shape=jax.ShapeDtypeStruct((bsz, t, h * MLA_V), BF16),
        grid=(bsz, h // 2, t // TM),
        in_specs=[pl.BlockSpec((1, TM, 2 * LANES), lambda b, g, j: (b, j, g)),
                  pl.BlockSpec((1, t, 2 * LANES), lambda b, g, j: (b, 0, g)),
                  pl.BlockSpec((1, t, LANES), lambda b, g, j: (b, 0, g))],
        out_specs=pl.BlockSpec((1, TM, LANES), lambda b, g, j: (b, j, g)),
        compiler_params=_cparams(("parallel", "parallel", "arbitrary")),
        name="mla_attn",
    )(q, k, v)
    return _oproj(x, o, mod, w_o.astype(BF16), 2)


def _moe_router_kernel(x_ref, g_ref, mod_ref, wr_ref, h_ref, lg_ref):
    h = _modnorm(x_ref[0], g_ref[...], mod_ref[0, 0, 3:4, :], mod_ref[0, 0, 4:5, :])
    h_ref[0] = h
    lg_ref[0] = lax.dot_general(wr_ref[...], h, (((1,), (1,)), ((), ())),
                                preferred_element_type=F32, precision=HIGHEST)


def _lane_cumsum_excl(x, tri):
    e, w = x.shape
    blk = tri.shape[0]
    carry = jnp.zeros((e, 1), F32)
    outs = []
    for i in range(w // blk):
        xb = x[:, i * blk:(i + 1) * blk]
        inc = jnp.dot(xb.astype(BF16), tri, preferred_element_type=F32)
        outs.append(inc - xb + carry)
        carry = carry + inc[:, blk - 1:blk]
    return outs[0] if len(outs) == 1 else jnp.concatenate(outs, axis=1)


def _moe_select_kernel(lg_ref, tri_ref, idx_ref, gate_ref, *, groups):
    lg = lg_ref[0]
    m = jnp.max(lg, axis=0, keepdims=True)
    ex = jnp.exp(lg - m)
    aff = ex / jnp.sum(ex, axis=0, keepdims=True)
    tri = tri_ref[...]
    ne = lg.shape[0]
    idx_ref[...] = jnp.zeros(idx_ref.shape, jnp.int32)
    gate_ref[...] = jnp.zeros(gate_ref.shape, F32)
    for (t0, tg, cap, s0) in groups:
        a = aff[:, t0:t0 + tg]
        bits = pltpu.bitcast(a, jnp.int32)
        thr = jnp.zeros((ne, 1), jnp.int32)
        for bit in range(30, -1, -1):
            cand = thr | (1 << bit)
            cnt = jnp.sum((bits >= cand).astype(jnp.int32), axis=1, keepdims=True)
            thr = jnp.where(cnt >= cap, cand, thr)
        gt = bits > thr
        eq = (bits == thr).astype(F32)
        need = cap - jnp.sum(gt.astype(jnp.int32), axis=1, keepdims=True)
        eq_rank = _lane_cumsum_excl(eq, tri)
        sel = jnp.where(gt, 1.0, jnp.where(eq_rank < need.astype(F32), eq, 0.0))
        pos = _lane_cumsum_excl(sel, tri)
        pos = jnp.where(sel > 0.0, pos, -1.0)
        tok = t0 + lax.broadcasted_iota(jnp.int32, (1, tg), 1)
        slot = lax.broadcasted_iota(jnp.int32, (cap, tg), 0).astype(F32)
        for e in range(ne):
            onehot = jnp.where(pos[e:e + 1, :] == slot, 1.0, 0.0).astype(BF16)
            ae = a[e:e + 1, :]
            a_hi = ae.astype(BF16)
            r1 = ae - a_hi.astype(F32)
            a_mid = r1.astype(BF16)
            a_lo = (r1 - a_mid.astype(F32)).astype(BF16)
            cols = jnp.concatenate([(tok // 64).astype(F32).astype(BF16), (tok % 64).astype(F32).astype(BF16),
                                    a_hi, a_mid, a_lo, jnp.zeros((3, tg), BF16)], axis=0)
            res = lax.dot_general(cols, onehot, (((1,), (1,)), ((), ())), preferred_element_type=F32)
            idx_ref[0, e:e + 1, s0:s0 + cap] = (res[0:1] * 64.0 + res[1:2]).astype(jnp.int32)
            gate_ref[0, e:e + 1, s0:s0 + cap] = (res[2:3] + res[3:4]) + res[4:5]


def _moe_ffn_kernel(idx_ref, h_hbm, w1_ref, w3_ref, w2_ref, y_ref, hbuf, xin, sem, *, slots):
    b = pl.program_id(0)

    @pl.when(pl.program_id(1) == 0)
    def _():
        cp = pltpu.make_async_copy(h_hbm.at[b], hbuf, sem)
        cp.start()
        cp.wait()

    def gather(c, carry):
        t = idx_ref[0, 0, 0, c]
        xin[pl.ds(c, 1), :] = hbuf[pl.ds(t, 1), :]
        return carry

    lax.fori_loop(0, slots, gather, 0, unroll=8)
    xb = xin[...].astype(BF16)
    h1 = jnp.dot(xb, w1_ref[0], preferred_element_type=F32)
    h3 = jnp.dot(xb, w3_ref[0], preferred_element_type=F32)
    hid = (h1 * _sigmoid(h1) * h3).astype(BF16)
    y_ref[0, 0] = jnp.dot(hid, w2_ref[0], preferred_element_type=F32)


def _moe_combine_kernel(idx_ref, gate_ref, y_ref, mod_ref, x_hbm, out_ref, sem, *, lat_slots, slots):
    b = pl.program_id(0)

    @pl.when(pl.program_id(1) == 0)
    def _():
        cp = pltpu.make_async_copy(x_hbm.at[b], out_ref.at[0], sem)
        cp.start()
        cp.wait()

    def scatter(grp):
        gvec = mod_ref[0, grp, 5:6, :]

        def body(c, carry):
            t = idx_ref[0, 0, 0, c]
            row = y_ref[0, 0, pl.ds(c, 1), :] * gate_ref[0, 0, 0, c]
            out_ref[0, pl.ds(t, 1), :] = out_ref[0, pl.ds(t, 1), :] + gvec * row
            return carry
        return body

    lax.fori_loop(0, lat_slots, scatter(1), 0, unroll=8)
    lax.fori_loop(lat_slots, slots, scatter(0), 0, unroll=8)


def _moe(x, mod, ctx_len, norm_g, w_router, w1_bf, w3_bf, w2_bf):
    bsz, t, d = x.shape
    n = t - ctx_len
    ne = w_router.shape[1]
    ff = w1_bf.shape[2]
    cap_l = max(1, EC_CAPACITY * n // ne)
    cap_c = max(1, EC_CAPACITY * ctx_len // ne)
    slots = cap_l + cap_c
    slot_pad = -(-slots // LANES) * LANES
    h, lg = pl.pallas_call(
        _moe_router_kernel,
        out_shape=[jax.ShapeDtypeStruct((bsz, t, d), F32), jax.ShapeDtypeStruct((bsz, ne, t), F32)],
        grid=(bsz, t // TM),
        in_specs=[_tok_spec(d), _const_spec((1, d)), _mod_spec(d), _const_spec((ne, d))],
        out_specs=[_tok_spec(d), pl.BlockSpec((1, ne, TM), lambda b, i: (b, 0, i))],
        compiler_params=_cparams(("parallel", "parallel")),
        name="moe_router",
    )(x, norm_g.reshape(1, d), mod, w_router.T)

    tri = jnp.asarray(np.triu(np.ones((256, 256), np.float32)), BF16)
    groups = ((ctx_len, n, cap_l, 0), (0, ctx_len, cap_c, cap_l))
    idx, gate = pl.pallas_call(
        functools.partial(_moe_select_kernel, groups=groups),
        out_shape=[jax.ShapeDtypeStruct((bsz, ne, slot_pad), jnp.int32),
                   jax.ShapeDtypeStruct((bsz, ne, slot_pad), F32)],
        grid=(bsz,),
        in_specs=[pl.BlockSpec((1, ne, t), lambda b: (b, 0, 0)), pl.BlockSpec((256, 256), lambda b: (0, 0))],
        out_specs=[pl.BlockSpec((1, ne, slot_pad), lambda b: (b, 0, 0))] * 2,
        compiler_params=_cparams(("parallel",)),
        name="moe_select",
    )(lg, tri)

    smem_spec = pl.BlockSpec((1, 1, 1, slot_pad), lambda b, e: (b, e, 0, 0), memory_space=pltpu.SMEM)
    idx = idx.reshape(bsz, ne, 1, slot_pad)
    gate = gate.reshape(bsz, ne, 1, slot_pad)
    y = pl.pallas_call(
        functools.partial(_moe_ffn_kernel, slots=slots),
        out_shape=jax.ShapeDtypeStruct((bsz, ne, slots, d), F32),
        grid=(bsz, ne),
        in_specs=[smem_spec, pl.BlockSpec(memory_space=pl.ANY),
                  pl.BlockSpec((1, d, ff), lambda b, e: (e, 0, 0)),
                  pl.BlockSpec((1, d, ff), lambda b, e: (e, 0, 0)),
                  pl.BlockSpec((1, ff, d), lambda b, e: (e, 0, 0))],
        out_specs=pl.BlockSpec((1, 1, slots, d), lambda b, e: (b, e, 0, 0)),
        scratch_shapes=[pltpu.VMEM((t, d), F32), pltpu.VMEM((slots, d), F32), pltpu.SemaphoreType.DMA],
        compiler_params=_cparams(("parallel", "arbitrary"), VMEM_BIG),
        name="moe_ffn",
    )(idx, h, w1_bf, w3_bf, w2_bf)

    return pl.pallas_call(
        functools.partial(_moe_combine_kernel, lat_slots=cap_l, slots=slots),
        out_shape=jax.ShapeDtypeStruct((bsz, t, d), F32),
        grid=(bsz, ne),
        in_specs=[smem_spec, smem_spec,
                  pl.BlockSpec((1, 1, slots, d), lambda b, e: (b, e, 0, 0)),
                  pl.BlockSpec((1, 2, 6, d), lambda b, e: (b, 0, 0, 0)),
                  pl.BlockSpec(memory_space=pl.ANY)],
        out_specs=pl.BlockSpec((1, t, d), lambda b, e: (b, 0, 0)),
        scratch_shapes=[pltpu.SemaphoreType.DMA],
        compiler_params=_cparams(("parallel", "arbitrary"), VMEM_BIG),
        name="moe_combine",
    )(idx, gate, y, mod, x)


def _mixer(i, x, mod, ctx_len, p):
    mixer, j = i % 4, i // 4
    if mixer == 0:
        return _layer_na(x, mod, ctx_len, p['norm1_g'][i], p['na_w_qkv'][j], p['na_q_g'][j], p['na_k_g'][j],
                         p['na_rpb'][j], p['na_w_o'][j])
    if mixer == 1:
        return _layer_rwkv(x, mod, ctx_len, p['norm1_g'][i], p['rw_mix'][j], p['rw_w_r'][j], p['rw_w_k'][j],
                           p['rw_w_v'][j], p['rw_w0'][j], p['rw_w1'][j], p['rw_w2'][j], p['rw_a0'][j], p['rw_a1'][j],
                           p['rw_a2'][j], p['rw_g1'][j], p['rw_g2'][j], p['rw_k_k'][j], p['rw_k_a'][j], p['rw_r_k'][j],
                           p['rw_ln_g'][j], p['rw_ln_b'][j], p['rw_w_o'][j])
    if mixer == 2:
        return _layer_mla(x, mod, ctx_len, p['norm1_g'][i], p['mla_w_down'][j], p['mla_q_norm_g'][j],
                          p['mla_kv_norm_g'][j], p['mla_w_uq'][j], p['mla_w_ukv'][j], p['mla_qn_g'][j],
                          p['mla_qr_g'][j], p['mla_kn_g'][j], p['mla_kr_g'][j], p['mla_w_o'][j])
    if mixer == 3:
        return _layer_swa(x, mod, ctx_len, p['norm1_g'][i], p['swa_w_qkv'][j], p['swa_q_g'][j], p['swa_k_g'][j],
                          p['swa_sink'][j], p['swa_w_o'][j])
    return x


_PARAM_NAMES = (
    'x', 'c', 'ctx', 'c_ctx', 'norm1_g', 'norm2_g', 'ada_w', 'ada_b',
    'na_w_qkv', 'na_q_g', 'na_k_g', 'na_rpb', 'na_w_o',
    'rw_mix', 'rw_w_r', 'rw_w_k', 'rw_w_v', 'rw_w0', 'rw_w1', 'rw_w2', 'rw_a0', 'rw_a1', 'rw_a2',
    'rw_g1', 'rw_g2', 'rw_k_k', 'rw_k_a', 'rw_r_k', 'rw_ln_g', 'rw_ln_b', 'rw_w_o',
    'mla_w_down', 'mla_q_norm_g', 'mla_kv_norm_g', 'mla_w_uq', 'mla_w_ukv',
    'mla_qn_g', 'mla_qr_g', 'mla_kn_g', 'mla_kr_g', 'mla_w_o',
    'swa_w_qkv', 'swa_q_g', 'swa_k_g', 'swa_sink', 'swa_w_o',
    'moe_router', 'moe_w1', 'moe_w3', 'moe_w2')


def kernel(*args):
    p = dict(zip(_PARAM_NAMES, args))
    ctx_len = p['ctx'].shape[1]
    assert ctx_len == TM
    depth = p['ada_w'].shape[0]
    mods = _ada_all(p['c'], p['c_ctx'], p['ada_w'], p['ada_b'])
    x = jnp.concatenate([p['ctx'], p['x']], axis=1)
    for i in range(depth):
        x = _mixer(i, x, mods[i], ctx_len, p)
        x = _moe(x, mods[i], ctx_len, p['norm2_g'][i], p['moe_router'][i], p['moe_w1'][i].astype(BF16),
                 p['moe_w3'][i].astype(BF16), p['moe_w2'][i].astype(BF16))
    return x[:, ctx_len:]
```

```python
import functools

import numpy as np
import jax
import jax.numpy as jnp
from jax import lax
from jax.experimental import pallas as pl
from jax.experimental.pallas import tpu as pltpu

F32 = jnp.float32
BF16 = jnp.bfloat16
HIGHEST = lax.Precision.HIGHEST

GRID_W = 64
ROPE_BASE = 10000.0
NORM_EPS = 1e-6
NEG_INF = -1e30
NA_HEADS = 16
NA_WIN_ROWS = 8
NA_WIN_COLS = 16
RW_HEAD = 64
RW_GN_EPS = 64e-5
MLA_HEADS = 16
MLA_Q_RANK = 384
MLA_KV_RANK = 256
MLA_NOPE = 64
MLA_ROPE = 32
MLA_V = 64
SWA_Q_HEADS = 16
SWA_KV_HEADS = 4
SWA_HEAD_DIM = 64
SWA_WINDOW = 128
N_EXPERTS = 16
EC_CAPACITY = 2

LANES = 128
TM = 256
VMEM_BIG = 56 * 1024 * 1024
VMEM_MID = 40 * 1024 * 1024


def _cparams(sem, vmem=VMEM_MID):
    return pltpu.CompilerParams(dimension_semantics=sem, vmem_limit_bytes=vmem)


def _tok_spec(width, tm=TM):
    return pl.BlockSpec((1, tm, width), lambda b, t: (b, t, 0))


def _const_spec(shape):
    nd = len(shape)
    return pl.BlockSpec(shape, lambda b, t: (0,) * nd)


def _mod_spec(d):
    return pl.BlockSpec((1, 1, 6, d), lambda b, t: (b, jnp.minimum(t, 1), 0, 0))


def _sigmoid(x):
    return 1.0 / (1.0 + jnp.exp(-x))


def _modnorm(x, g, shift, scale):
    ms = jnp.mean(x * x, axis=-1, keepdims=True)
    y = x * lax.rsqrt(ms + NORM_EPS) * g
    return y * (1.0 + scale) + shift


def _group_sums(x2, gmat):
    hi = x2.astype(BF16)
    lo = (x2 - hi.astype(F32)).astype(BF16)
    return (jnp.dot(hi, gmat, preferred_element_type=F32)
            + jnp.dot(lo, gmat, preferred_element_type=F32))


def _group_rms(x, gmat, inv_cnt, eps):
    outs = []
    for c in range(x.shape[1] // LANES):
        xc = x[:, c * LANES:(c + 1) * LANES]
        ss = _group_sums(xc * xc, gmat)
        outs.append(xc * lax.rsqrt(ss * inv_cnt + eps))
    return outs[0] if len(outs) == 1 else jnp.concatenate(outs, axis=1)


def _rope_swap(x, half):
    w = x.shape[1]
    lane = lax.broadcasted_iota(jnp.int32, x.shape, 1)
    fwd = pltpu.roll(x, w - half, 1)
    bwd = pltpu.roll(x, half, 1)
    return jnp.where((lane % (2 * half)) < half, fwd, bwd)


def _block_ones(sizes):
    m = np.zeros((LANES, LANES), np.float32)
    o = 0
    for s, on in sizes:
        if on:
            m[o:o + s, o:o + s] = 1.0
        o += s
    return jnp.asarray(m, BF16)


def _ada_kernel(cond_ref, w_ref, b_ref, o_ref):
    cnd = cond_ref[...]
    s = cnd * _sigmoid(cnd)
    o_ref[0] = jnp.dot(s, w_ref[0], preferred_element_type=F32, precision=HIGHEST) + b_ref[0]


def _ada_all(c, c_ctx, ada_w, ada_b):
    depth, d, d6 = ada_w.shape
    bsz = c.shape[0]
    rows = 16
    cond = jnp.zeros((rows, d), F32).at[:bsz].set(c).at[bsz].set(c_ctx)
    tn = 1536
    m = pl.pallas_call(
        _ada_kernel,
        out_shape=jax.ShapeDtypeStruct((depth, rows, d6), F32),
        grid=(depth, d6 // tn),
        in_specs=[pl.BlockSpec((rows, d), lambda i, j: (0, 0)),
                  pl.BlockSpec((1, d, tn), lambda i, j: (i, 0, j)),
                  pl.BlockSpec((1, 1, tn), lambda i, j: (i, 0, j))],
        out_specs=pl.BlockSpec((1, rows, tn), lambda i, j: (i, 0, j)),
        compiler_params=_cparams(("parallel", "parallel")),
        name="ada_mod",
    )(cond, ada_w, ada_b.reshape(depth, 1, d6))
    lat = m[:, :bsz].reshape(depth, bsz, 6, d)
    ctx = jnp.broadcast_to(m[:, bsz].reshape(depth, 1, 6, d), (depth, bsz, 6, d))
    return jnp.stack([ctx, lat], axis=2)


def _oproj_kernel(x_ref, o_ref, mod_ref, w_ref, out_ref, *, gate_row):
    y = jnp.dot(o_ref[0], w_ref[...], preferred_element_type=F32)
    out_ref[0] = x_ref[0] + mod_ref[0, 0, gate_row:gate_row + 1, :] * y


def _oproj(x, o, mod, w_bf16, gate_row):
    bsz, t, d = x.shape
    din = o.shape[2]
    return pl.pallas_call(
        functools.partial(_oproj_kernel, gate_row=gate_row),
        out_shape=jax.ShapeDtypeStruct(x.shape, F32),
        grid=(bsz, t // TM),
        in_specs=[_tok_spec(d), _tok_spec(din), _mod_spec(d), _const_spec((din, d))],
        out_specs=_tok_spec(d),
        compiler_params=_cparams(("parallel", "parallel")),
        name="oproj",
    )(x, o, mod, w_bf16)


def _na_proj_kernel(x_ref, g_ref, mod_ref, w_ref, gq_ref, gk_ref, gm_ref, q_ref, k_ref, v_ref, *, scale):
    d = x_ref.shape[2]
    hn = _modnorm(x_ref[0], g_ref[...], mod_ref[0, 0, 0:1, :], mod_ref[0, 0, 1:2, :]).astype(BF16)
    qkv = jnp.dot(hn, w_ref[...], preferred_element_type=F32)
    inv = 1.0 / (d // NA_HEADS)
    q = _group_rms(qkv[:, :d], gm_ref[...], inv, NORM_EPS) * (gq_ref[...] * scale)
    k = _group_rms(qkv[:, d:2 * d], gm_ref[...], inv, NORM_EPS) * gk_ref[...]
    q_ref[0] = q.astype(BF16)
    k_ref[0] = k.astype(BF16)
    v_ref[0] = qkv[:, 2 * d:].astype(BF16)


def _na_attn_kernel(q_ref, k_ref, v_ref, bias_ref, o_ref, *, ctx_len, rows, qblk_per_ctx):
    j = pl.program_id(1)
    r = jnp.maximum(j - qblk_per_ctx, 0)
    r0 = jnp.clip(r - NA_WIN_ROWS // 2, 0, rows - NA_WIN_ROWS)
    ws = pl.multiple_of(ctx_len + GRID_W * r0, GRID_W)
    nwin = NA_WIN_ROWS * GRID_W
    lane = lax.broadcasted_iota(jnp.int32, (GRID_W, LANES), 1)
    dn = (((1,), (1,)), ((), ()))
    for g in range(q_ref.shape[2] // LANES):
        cs = slice(g * LANES, (g + 1) * LANES)
        q2 = q_ref[0, :, cs]
        kw = k_ref[0, pl.ds(ws, nwin), cs]
        vw = v_ref[0, pl.ds(ws, nwin), cs]
        kc = k_ref[0, 0:ctx_len, cs]
        vc = v_ref[0, 0:ctx_len, cs]
        zq = jnp.zeros_like(q2)
        qs = jnp.concatenate([jnp.where(lane < LANES // 2, q2, zq), jnp.where(lane < LANES // 2, zq, q2)], axis=0)
        bias = jnp.concatenate([bias_ref[0, 2 * g], bias_ref[0, 2 * g + 1]], axis=0)
        s1 = lax.dot_general(qs, kw, dn, preferred_element_type=F32) + bias
        s2 = lax.dot_general(qs, kc, dn, preferred_element_type=F32)
        m = jnp.maximum(jnp.max(s1, axis=-1, keepdims=True), jnp.max(s2, axis=-1, keepdims=True))
        p1 = jnp.exp(s1 - m)
        p2 = jnp.exp(s2 - m)
        den = jnp.sum(p1, axis=-1, keepdims=True) + jnp.sum(p2, axis=-1, keepdims=True)
        o = (jnp.dot(p1.astype(BF16), vw, preferred_element_type=F32)
             + jnp.dot(p2.astype(BF16), vc, preferred_element_type=F32)) / den
        o_ref[0, :, cs] = jnp.where(lane < LANES // 2, o[:GRID_W], o[GRID_W:]).astype(BF16)


def _na_bias_table(rpb):
    h = rpb.shape[0]
    col = np.arange(GRID_W)
    cstart = np.clip(col - NA_WIN_COLS // 2, 0, GRID_W - NA_WIN_COLS)
    ok = (col[None, :] >= cstart[:, None]) & (col[None, :] < cstart[:, None] + NA_WIN_COLS)
    padc = GRID_W - NA_WIN_COLS
    rp = jnp.pad(rpb.astype(F32), ((0, 0), (0, 0), (padc, padc)))
    by_c = jnp.stack([rp[:, :, GRID_W - 1 - c:2 * GRID_W - 1 - c] for c in range(GRID_W)], axis=1)
    t = jnp.stack([by_c[:, :, NA_WIN_ROWS - 1 - dl:2 * NA_WIN_ROWS - 1 - dl] for dl in range(NA_WIN_ROWS)],
                  axis=0)
    t = jnp.where(jnp.asarray(ok)[None, None, :, None, :], t, NEG_INF)
    t = t.reshape(NA_WIN_ROWS, h, GRID_W, NA_WIN_ROWS * GRID_W)
    return jnp.concatenate([t, jnp.full((1,) + t.shape[1:], NEG_INF, F32)], axis=0)


def _layer_na(x, mod, ctx_len, norm_g, w_qkv, q_g, k_g, rpb, w_o):
    bsz, t, d = x.shape
    n = t - ctx_len
    rows = n // GRID_W
    dh = d // NA_HEADS
    gm = _block_ones([(dh, True), (dh, True)])
    q, k, v = pl.pallas_call(
        functools.partial(_na_proj_kernel, scale=dh ** -0.5),
        out_shape=[jax.ShapeDtypeStruct((bsz, t, d), BF16)] * 3,
        grid=(bsz, t // TM),
        in_specs=[_tok_spec(d), _const_spec((1, d)), _mod_spec(d), _const_spec((d, 3 * d)),
                  _const_spec((1, d)), _const_spec((1, d)), _const_spec((LANES, LANES))],
        out_specs=[_tok_spec(d)] * 3,
        compiler_params=_cparams(("parallel", "parallel")),
        name="na_proj",
    )(x, norm_g.reshape(1, d), mod, w_qkv.astype(BF16),
      jnp.tile(q_g, NA_HEADS).reshape(1, d), jnp.tile(k_g, NA_HEADS).reshape(1, d), gm)

    table = _na_bias_table(rpb)
    qpc = ctx_len // GRID_W

    def bias_idx(b, j):
        r = jnp.maximum(j - qpc, 0)
        delta = r - jnp.clip(r - NA_WIN_ROWS // 2, 0, rows - NA_WIN_ROWS)
        return (jnp.where(j < qpc, NA_WIN_ROWS, delta), 0, 0, 0)

    o = pl.pallas_call(
        functools.partial(_na_attn_kernel, ctx_len=ctx_len, rows=rows, qblk_per_ctx=qpc),
        out_shape=jax.ShapeDtypeStruct((bsz, t, d), BF16),
        grid=(bsz, t // GRID_W),
        in_specs=[pl.BlockSpec((1, GRID_W, d), lambda b, j: (b, j, 0)),
                  pl.BlockSpec((1, t, d), lambda b, j: (b, 0, 0)),
                  pl.BlockSpec((1, t, d), lambda b, j: (b, 0, 0)),
                  pl.BlockSpec((1, NA_HEADS, GRID_W, NA_WIN_ROWS * GRID_W), bias_idx)],
        out_specs=pl.BlockSpec((1, GRID_W, d), lambda b, j: (b, j, 0)),
        compiler_params=_cparams(("parallel", "arbitrary"), VMEM_BIG),
        name="na_attn",
    )(q, k, v, table)
    return _oproj(x, o, mod, w_o.astype(BF16), 2)


RW_PAIRS = 8
RW_CHUNK = 64


def _rw_proj_kernel(x_ref, xp_ref, xn_ref, g_ref, mod_ref, mix_ref, wr_ref, wk_ref, wv_ref, g1_ref, g2_ref,
                    w1_ref, w2_ref, w0_ref, a1_ref, a2_ref, a0_ref, kk_g_ref, ka_ref, gm_ref,
                    r_ref, v_ref, kk_ref, gt_ref, lw_ref, kka_ref, kt_ref):
    t = pl.program_id(1)
    nt = pl.num_programs(1)
    g = g_ref[...]
    shift, scale = mod_ref[0, 0, 0:1, :], mod_ref[0, 0, 1:2, :]
    h = _modnorm(x_ref[0], g, shift, scale)
    tm = h.shape[0]
    hp8 = _modnorm(xp_ref[0], g, shift, scale)
    hn8 = _modnorm(xn_ref[0], g, shift, scale)
    prev_row = jnp.where(t >= 2, hp8[7:8, :], 0.0)
    next_row = jnp.where(jnp.logical_and(t >= 1, t < nt - 1), hn8[0:1, :], 0.0)
    row = lax.broadcasted_iota(jnp.int32, h.shape, 0)
    h_prev = jnp.where(row == 0, prev_row, pltpu.roll(h, 1, 0))
    h_next = jnp.where(row == tm - 1, next_row, pltpu.roll(h, tm - 1, 0))
    xx = 0.5 * (h_prev + h_next) - h

    def mixed(s):
        return (h + xx * mix_ref[s:s + 1, :]).astype(BF16)

    r = jnp.dot(mixed(0), wr_ref[...], preferred_element_type=F32)
    k = jnp.dot(mixed(2), wk_ref[...], preferred_element_type=F32)
    v = jnp.dot(mixed(3), wv_ref[...], preferred_element_type=F32)
    gl = _sigmoid(jnp.dot(mixed(5), g1_ref[...], preferred_element_type=F32)).astype(BF16)
    gt_ref[0] = jnp.dot(gl, g2_ref[...], preferred_element_type=F32)
    dlo = jnp.tanh(jnp.dot(mixed(1), w1_ref[...], preferred_element_type=F32)).astype(BF16)
    alo = jnp.dot(mixed(4), a1_ref[...], preferred_element_type=F32).astype(BF16)
    kk = k * kk_g_ref[...]
    nrm = jnp.concatenate([jnp.sqrt(_group_sums(kk[:, c * LANES:(c + 1) * LANES] ** 2, gm_ref[...]))
                           for c in range(kk.shape[1] // LANES)], axis=1)
    kk = kk / jnp.maximum(nrm, 1e-12)
    r_ref[0] = r
    v_ref[0] = v
    kk_ref[0] = kk
    for z in range(2):
        dec = w0_ref[z:z + 1, :] + jnp.dot(dlo, w2_ref[z], preferred_element_type=F32)
        sp = jnp.maximum(-dec, 0.0) + jnp.log(1.0 + jnp.exp(-jnp.abs(dec)))
        lw_ref[z, 0] = -jnp.exp(-sp - 0.5)
        a = _sigmoid(a0_ref[z:z + 1, :] + jnp.dot(alo, a2_ref[z], preferred_element_type=F32))
        kka_ref[z, 0] = kk * a
        kt_ref[z, 0] = k * (1.0 + (a - 1.0) * ka_ref[...])


def _rw_scan_kernel(lw_ref, kk_ref, kka_ref, kt_ref, r_ref, v_ref, y_ref, s_ref, *, reverse, pairs):
    c = RW_CHUNK
    hd = RW_HEAD
    pw = 2 * hd

    @pl.when(pl.program_id(2) == 0)
    def _():
        s_ref[...] = jnp.zeros(s_ref.shape, F32)

    def split(a):
        hi = a.astype(BF16)
        return hi, (a - hi.astype(F32)).astype(BF16)

    def dot3(a, b, dn):
        return (lax.dot_general(a[0], b[0], dn, preferred_element_type=F32)
                + lax.dot_general(a[0], b[1], dn, preferred_element_type=F32)
                + lax.dot_general(a[1], b[0], dn, preferred_element_type=F32))

    def mm(a, b):
        return dot3(a, b, (((1,), (0,)), ((), ())))

    def mm_nt(a, b):
        return dot3(a, b, (((1,), (1,)), ((), ())))

    rr = lax.broadcasted_iota(jnp.int32, (c, c), 0)
    cc = lax.broadcasted_iota(jnp.int32, (c, c), 1)
    before = (cc > rr) if reverse else (cc < rr)
    strict = before.astype(F32)
    incl = jnp.logical_or(before, cc == rr).astype(F32)
    eye = (cc == rr).astype(F32)

    lw = lw_ref[0, 0]
    cl = mm(split(incl), split(lw))
    e_in = jnp.exp(cl)
    e_inv = jnp.exp(-cl)
    pt_all = kk_ref[0] * jnp.exp(cl - lw)
    rt_all = r_ref[0] * e_in
    qt_all = kka_ref[0, 0] * e_inv
    kt_all = kt_ref[0, 0] * e_inv
    v_all = v_ref[0]
    last = 0 if reverse else c - 1

    lane = lax.broadcasted_iota(jnp.int32, (c, pw), 1)
    ri = lax.broadcasted_iota(jnp.int32, (pw, pw), 0)
    ci = lax.broadcasted_iota(jnp.int32, (pw, pw), 1)
    same = (ri < hd) == (ci < hd)
    gps = range(pairs)
    heads = [(gp, a) for gp in gps for a in range(2)]
    lss = [slice(gp * pw, (gp + 1) * pw) for gp in gps]
    qt = [split(qt_all[:, ls]) for ls in lss]
    kt = [split(kt_all[:, ls]) for ls in lss]
    s2f = [s_ref[gp] for gp in gps]
    s2 = [split(x) for x in s2f]

    def own(a):
        return (lane >= hd) if a else (lane < hd)

    pa = [split(jnp.where(own(a), pt_all[:, lss[gp]], 0.0)) for gp, a in heads]
    ra = [split(jnp.where(own(a), rt_all[:, lss[gp]], 0.0)) for gp, a in heads]
    va = [split(jnp.where(own(a), v_all[:, lss[gp]], 0.0)) for gp, a in heads]
    npow = [-(mm_nt(pa[h], qt[gp]) * strict) for h, (gp, a) in enumerate(heads)]
    inv = [eye + x for x in npow]
    for _ in range(5):
        nps = [split(x) for x in npow]
        npow = [mm(x, x) for x in nps]
        inv = [i + mm(split(i), split(x)) for i, x in zip(inv, npow)]
    a_pk = [split(mm_nt(pa[h], kt[gp]) * strict) for h, (gp, a) in enumerate(heads)]
    rhs = [mm_nt(pa[h], s2[gp]) + mm(a_pk[h], va[h]) for h, (gp, a) in enumerate(heads)]
    ua = [mm(split(i), split(x)) for i, x in zip(inv, rhs)]
    a_rq = [split(mm_nt(ra[h], qt[gp]) * incl) for h, (gp, a) in enumerate(heads)]
    a_rk = [split(mm_nt(ra[h], kt[gp]) * incl) for h, (gp, a) in enumerate(heads)]
    ya = [mm_nt(ra[h], s2[gp]) - mm(a_rq[h], split(ua[h])) + mm(a_rk[h], va[h])
          for h, (gp, a) in enumerate(heads)]
    for gp in gps:
        ls = lss[gp]
        y_ref[0, :, ls] = ya[2 * gp] + ya[2 * gp + 1]
        u2 = ua[2 * gp] + ua[2 * gp + 1]
        g_last = e_in[last:last + 1, ls]
        upd = s2f[gp] - mm(split(u2.T), qt[gp]) + mm(split(v_all[:, ls].T), kt[gp])
        s_ref[gp] = jnp.where(same, upd, 0.0) * g_last


def _rw_out_kernel(x_ref, yf_ref, yb_ref, r_ref, v_ref, gt_ref, kt_ref, mod_ref, lng_ref, lnb_ref, rk_ref,
                   gm_ref, wo_ref, out_ref):
    y = yf_ref[0] + yb_ref[0]
    r = r_ref[0]
    brk = (kt_ref[0, 0] + kt_ref[1, 0]) * r * rk_ref[...]
    inv = 1.0 / RW_HEAD
    cols = []
    for c in range(y.shape[1] // LANES):
        cs = slice(c * LANES, (c + 1) * LANES)
        yc = y[:, cs]
        mu = _group_sums(yc, gm_ref[...]) * inv
        dv = yc - mu
        var = _group_sums(dv * dv, gm_ref[...]) * inv
        yn = dv * lax.rsqrt(var + RW_GN_EPS) * lng_ref[:, cs] + lnb_ref[:, cs]
        bonus = _group_sums(brk[:, cs], gm_ref[...]) * v_ref[0, :, cs]
        cols.append(yn + bonus)
    o = (jnp.concatenate(cols, axis=1) * gt_ref[0]).astype(BF16)
    out_ref[0] = x_ref[0] + mod_ref[0, 0, 2:3, :] * jnp.dot(o, wo_ref[...], preferred_element_type=F32)


def _layer_rwkv(x, mod, ctx_len, norm_g, mix, w_r, w_k, w_v, w0, w1, w2, a0, a1, a2, g1, g2,
                k_k, k_a, r_k, ln_g, ln_b, w_o):
    bsz, t, d = x.shape
    hd = RW_HEAD
    lora_d, lora_a, lora_g = w1.shape[2], a1.shape[2], g1.shape[1]

    def cat_in(w):
        return jnp.concatenate([w[0], w[1]], axis=1).astype(BF16)

    def pad_out(w):
        z = jnp.zeros_like(w[0])
        return jnp.stack([jnp.concatenate([w[0], z], axis=0), jnp.concatenate([z, w[1]], axis=0)]).astype(BF16)

    gm = _block_ones([(hd, True), (hd, True)])
    nt = t // TM
    sub = TM // 8
    tok = jax.ShapeDtypeStruct((bsz, t, d), F32)
    tok2 = jax.ShapeDtypeStruct((2, bsz, t, d), F32)
    spec2 = pl.BlockSpec((2, 1, TM, d), lambda b, i: (0, b, i, 0))
    r, v, kk, gt, lw, kka, kt = pl.pallas_call(
        _rw_proj_kernel,
        out_shape=[tok, tok, tok, tok, tok2, tok2, tok2],
        grid=(bsz, nt),
        in_specs=[_tok_spec(d),
                  pl.BlockSpec((1, 8, d), lambda b, i: (b, jnp.maximum(i * sub - 1, 0), 0)),
                  pl.BlockSpec((1, 8, d), lambda b, i: (b, jnp.minimum((i + 1) * sub, nt * sub - 1), 0)),
                  _const_spec((1, d)), _mod_spec(d), _const_spec((6, d)),
                  _const_spec((d, d)), _const_spec((d, d)), _const_spec((d, d)),
                  _const_spec((d, lora_g)), _const_spec((lora_g, d)),
                  _const_spec((d, 2 * lora_d)), _const_spec((2, 2 * lora_d, d)), _const_spec((2, d)),
                  _const_spec((d, 2 * lora_a)), _const_spec((2, 2 * lora_a, d)), _const_spec((2, d)),
                  _const_spec((1, d)), _const_spec((1, d)), _const_spec((LANES, LANES))],
        out_specs=[_tok_spec(d)] * 4 + [spec2] * 3,
        compiler_params=_cparams(("parallel", "parallel"), VMEM_BIG),
        name="rw_proj",
    )(x, x, x, norm_g.reshape(1, d), mod, mix, w_r.astype(BF16), w_k.astype(BF16), w_v.astype(BF16),
      g1.astype(BF16), g2.astype(BF16), cat_in(w1), pad_out(w2), w0, cat_in(a1), pad_out(a2), a0,
      k_k.reshape(1, d), k_a.reshape(1, d), gm)

    nc = t // RW_CHUNK
    ncc = ctx_len // RW_CHUNK
    pairs = RW_PAIRS
    def scan_specs(z, reverse):
        def chunk(s):
            return jnp.where(s < ncc, ncc - 1 - s, nc - 1 - s + ncc) if reverse else s

        def dir_map(b, g, s):
            return (z, b, chunk(s), g)

        def sh_map(b, g, s):
            return (b, chunk(s), g)

        return (pl.BlockSpec((1, 1, RW_CHUNK, 2 * hd * pairs), dir_map),
                pl.BlockSpec((1, RW_CHUNK, 2 * hd * pairs), sh_map))

    ys = []
    for z, reverse in enumerate((False, True)):
        dir_spec, sh_spec = scan_specs(z, reverse)
        ys.append(pl.pallas_call(
            functools.partial(_rw_scan_kernel, reverse=reverse, pairs=pairs),
            out_shape=tok,
            grid=(bsz, d // (2 * hd * pairs), nc),
            in_specs=[dir_spec, sh_spec, dir_spec, dir_spec, sh_spec, sh_spec],
            out_specs=sh_spec,
            scratch_shapes=[pltpu.VMEM((pairs, 2 * hd, 2 * hd), F32)],
            compiler_params=_cparams(("parallel", "parallel", "arbitrary")),
            name="rw_scan_bwd" if reverse else "rw_scan_fwd",
        )(lw, kk, kka, kt, r, v))

    return pl.pallas_call(
        _rw_out_kernel,
        out_shape=tok,
        grid=(bsz, nt),
        in_specs=[_tok_spec(d)] * 6 + [spec2, _mod_spec(d), _const_spec((1, d)), _const_spec((1, d)),
                                       _const_spec((1, d)), _const_spec((LANES, LANES)), _const_spec((d, d))],
        out_specs=_tok_spec(d),
        compiler_params=_cparams(("parallel", "parallel")),
        name="rw_out",
    )(x, ys[0], ys[1], r, v, gt, kt, mod, ln_g.reshape(1, d), ln_b.reshape(1, d), r_k.reshape(1, d), gm,
      w_o.astype(BF16))


def _axial_tables(n, ctx_len, d_rot, lead, reps):
    t = jnp.arange(n)
    row = (t // GRID_W).astype(F32)
    col = (t % GRID_W).astype(F32)
    d_axis = d_rot // 2
    inv = ROPE_BASE ** (-jnp.arange(0, d_axis, 2, dtype=F32) / d_axis)
    ang = jnp.concatenate([row[:, None] * inv, col[:, None] * inv], axis=-1)
    cos, sin = jnp.cos(ang), jnp.sin(ang)
    tail = LANES - lead - reps * d_rot
    cos_t = jnp.concatenate([jnp.ones((n, lead), F32)] + [cos, cos] * reps + [jnp.ones((n, tail), F32)], axis=1)
    sin_t = jnp.concatenate([jnp.zeros((n, lead), F32)] + [-sin, sin] * reps + [jnp.zeros((n, tail), F32)], axis=1)
    cos_t = jnp.concatenate([jnp.ones((ctx_len, LANES), F32), cos_t], axis=0)
    sin_t = jnp.concatenate([jnp.zeros((ctx_len, LANES), F32), sin_t], axis=0)
    return cos_t, sin_t


def _tab_spec():
    return pl.BlockSpec((TM, LANES), lambda b, t: (t, 0))


def _swa_proj_kernel(x_ref, g_ref, mod_ref, w_ref, gq_ref, gk_ref, gm_ref, cos_ref, sin_ref,
                     q_ref, k_ref, v_ref, *, nq, nk, scale):
    hn = _modnorm(x_ref[0], g_ref[...], mod_ref[0, 0, 0:1, :], mod_ref[0, 0, 1:2, :]).astype(BF16)
    qkv = jnp.dot(hn, w_ref[...], preferred_element_type=F32)
    cos, sin = cos_ref[...], sin_ref[...]
    inv = 1.0 / SWA_HEAD_DIM
    half = SWA_HEAD_DIM // 2

    def norm_rope(xc, gain):
        y = xc * lax.rsqrt(_group_sums(xc * xc, gm_ref[...]) * inv + NORM_EPS) * gain
        return y * cos + _rope_swap(y, half) * sin

    for c in range(nq // LANES):
        cs = slice(c * LANES, (c + 1) * LANES)
        q_ref[0, :, cs] = (norm_rope(qkv[:, cs], gq_ref[:, cs]) * scale).astype(BF16)
    for c in range(nk // LANES):
        cs = slice(c * LANES, (c + 1) * LANES)
        k_ref[0, :, cs] = norm_rope(qkv[:, nq + c * LANES:nq + (c + 1) * LANES], gk_ref[:, cs]).astype(BF16)
    v_ref[0] = qkv[:, nq + nk:].astype(BF16)


def _swa_attn_kernel(sink_ref, q_ref, k_ref, v_ref, o_ref, *, ctx_len, qb):
    t_total = k_ref.shape[1]
    j = pl.program_id(1)
    q0 = j * qb
    nloc = 3 * qb
    ks = pl.multiple_of(jnp.clip(q0 - qb, 0, t_total - nloc), qb)
    group = SWA_Q_HEADS // SWA_KV_HEADS
    qpos = q0 + lax.broadcasted_iota(jnp.int32, (group * qb, nloc), 0) % qb
    kpos = ks + lax.broadcasted_iota(jnp.int32, (group * qb, nloc), 1)
    ok = jnp.where(kpos >= ctx_len, jnp.where(qpos >= ctx_len, jnp.abs(qpos - kpos), SWA_WINDOW + 1), SWA_WINDOW + 1)
    ok = ok <= SWA_WINDOW
    lane = lax.broadcasted_iota(jnp.int32, (qb, LANES), 1)
    dn = (((1,), (1,)), ((), ()))
    group = SWA_Q_HEADS // SWA_KV_HEADS
    ok_g = ok
    for hk in range(SWA_KV_HEADS):
        qs, snks = [], []
        for gi in range(group):
            h = hk * group + gi
            q2 = q_ref[0, :, (h // 2) * LANES:(h // 2 + 1) * LANES]
            qs.append(jnp.where((lane < LANES // 2) == (h % 2 == 0), q2, jnp.zeros_like(q2)))
            snks.append(jnp.full((qb, 1), sink_ref[h], F32))
        qs = jnp.concatenate(qs, axis=0)
        snk = jnp.concatenate(snks, axis=0)
        kcs = slice(hk * LANES, (hk + 1) * LANES)
        kw = k_ref[0, pl.ds(ks, nloc), kcs]
        vw = v_ref[0, pl.ds(ks, nloc), kcs]
        kc = k_ref[0, 0:ctx_len, kcs]
        vc = v_ref[0, 0:ctx_len, kcs]
        s1 = jnp.where(ok_g, lax.dot_general(qs, kw, dn, preferred_element_type=F32), NEG_INF)
        s2 = lax.dot_general(qs, kc, dn, preferred_element_type=F32)
        m = jnp.maximum(jnp.maximum(jnp.max(s1, axis=-1, keepdims=True), jnp.max(s2, axis=-1, keepdims=True)), snk)
        p1 = jnp.exp(s1 - m)
        p2 = jnp.exp(s2 - m)
        den = jnp.sum(p1, axis=-1, keepdims=True) + jnp.sum(p2, axis=-1, keepdims=True) + jnp.exp(snk - m)
        o = (jnp.dot(p1.astype(BF16), vw, preferred_element_type=F32)
             + jnp.dot(p2.astype(BF16), vc, preferred_element_type=F32)) / den
        for gi in range(0, group, 2):
            c = (hk * group + gi) // 2
            o_ref[0, :, c * LANES:(c + 1) * LANES] = jnp.where(
                lane < LANES // 2, o[gi * qb:(gi + 1) * qb], o[(gi + 1) * qb:(gi + 2) * qb]).astype(BF16)


def _layer_swa(x, mod, ctx_len, norm_g, w_qkv, q_g, k_g, sink, w_o):
    bsz, t, d = x.shape
    n = t - ctx_len
    dh = SWA_HEAD_DIM
    nq = SWA_Q_HEADS * dh
    nkv = SWA_KV_HEADS * dh

    def dup(w):
        return jnp.broadcast_to(w.reshape(d, SWA_KV_HEADS, 1, dh), (d, SWA_KV_HEADS, 2, dh)).reshape(d, 2 * nkv)

    w_ext = jnp.concatenate([w_qkv[:, :nq], dup(w_qkv[:, nq:nq + nkv]), dup(w_qkv[:, nq + nkv:])], axis=1).astype(BF16)
    cos_t, sin_t = _axial_tables(n, ctx_len, dh, 0, LANES // dh)
    gm = _block_ones([(dh, True), (dh, True)])
    q, k, v = pl.pallas_call(
        functools.partial(_swa_proj_kernel, nq=nq, nk=2 * nkv, scale=dh ** -0.5),
        out_shape=[jax.ShapeDtypeStruct((bsz, t, nq), BF16), jax.ShapeDtypeStruct((bsz, t, 2 * nkv), BF16),
                   jax.ShapeDtypeStruct((bsz, t, 2 * nkv), BF16)],
        grid=(bsz, t // TM),
        in_specs=[_tok_spec(d), _const_spec((1, d)), _mod_spec(d), _const_spec((d, nq + 4 * nkv)),
                  _const_spec((1, nq)), _const_spec((1, 2 * nkv)), _const_spec((LANES, LANES)),
                  _tab_spec(), _tab_spec()],
        out_specs=[_tok_spec(nq), _tok_spec(2 * nkv), _tok_spec(2 * nkv)],
        compiler_params=_cparams(("parallel", "parallel")),
        name="swa_proj",
    )(x, norm_g.reshape(1, d), mod, w_ext, jnp.tile(q_g, SWA_Q_HEADS).reshape(1, nq),
      jnp.tile(k_g, 2 * SWA_KV_HEADS).reshape(1, 2 * nkv), gm, cos_t, sin_t)

    qb = SWA_WINDOW
    o = pl.pallas_call(
        functools.partial(_swa_attn_kernel, ctx_len=ctx_len, qb=qb),
        out_shape=jax.ShapeDtypeStruct((bsz, t, nq), BF16),
        grid=(bsz, t // qb),
        in_specs=[pl.BlockSpec(memory_space=pltpu.SMEM),
                  pl.BlockSpec((1, qb, nq), lambda b, j: (b, j, 0)),
                  pl.BlockSpec((1, t, 2 * nkv), lambda b, j: (b, 0, 0)),
                  pl.BlockSpec((1, t, 2 * nkv), lambda b, j: (b, 0, 0))],
        out_specs=pl.BlockSpec((1, qb, nq), lambda b, j: (b, j, 0)),
        compiler_params=_cparams(("parallel", "arbitrary")),
        name="swa_attn",
    )(sink.astype(F32), q, k, v)
    return _oproj(x, o, mod, w_o.astype(BF16), 2)


def _mla_proj_kernel(x_ref, g_ref, mod_ref, wd_ref, gqn_ref, gkvn_ref, gkr_ref, wuq_ref, wukn_ref, wuv_ref,
                     gq_ref, gk_ref, gm_ref, invc_ref, cos_ref, sin_ref, q_ref, k_ref, v_ref, *, scale):
    hn = _modnorm(x_ref[0], g_ref[...], mod_ref[0, 0, 0:1, :], mod_ref[0, 0, 1:2, :]).astype(BF16)
    dd = jnp.dot(hn, wd_ref[...], preferred_element_type=F32)
    cos, sin = cos_ref[...], sin_ref[...]
    half = MLA_ROPE // 2

    def rms(z, g):
        return z * lax.rsqrt(jnp.mean(z * z, axis=-1, keepdims=True) + NORM_EPS) * g

    def rope(y):
        return y * cos + _rope_swap(y, half) * sin

    cq = rms(dd[:, :MLA_Q_RANK], gqn_ref[...]).astype(BF16)
    ckv = rms(dd[:, MLA_Q_RANK:MLA_Q_RANK + MLA_KV_RANK], gkvn_ref[...]).astype(BF16)
    kr = dd[:, MLA_Q_RANK + MLA_KV_RANK:]
    kr = kr * lax.rsqrt(jnp.sum(kr * kr, axis=-1, keepdims=True) * (1.0 / MLA_ROPE) + NORM_EPS) * gkr_ref[...]
    kr = rope(kr)
    q = jnp.dot(cq, wuq_ref[...], preferred_element_type=F32)
    kn = jnp.dot(ckv, wukn_ref[...], preferred_element_type=F32)
    invc = invc_ref[...]
    for c in range(MLA_HEADS):
        cs = slice(c * LANES, (c + 1) * LANES)
        xc = q[:, cs]
        y = xc * lax.rsqrt(_group_sums(xc * xc, gm_ref[...]) * invc + NORM_EPS) * gq_ref[...]
        q_ref[0, :, cs] = (rope(y) * scale).astype(BF16)
        xk = kn[:, cs]
        yk = xk * lax.rsqrt(_group_sums(xk * xk, gm_ref[...]) * invc + NORM_EPS) * gk_ref[...]
        k_ref[0, :, cs] = (yk + kr).astype(BF16)
    v_ref[0] = jnp.dot(ckv, wuv_ref[...], preferred_element_type=F32).astype(BF16)


def _mla_attn_kernel(q_ref, k_ref, v_ref, o_ref, *, ctx_len):
    t_total = k_ref.shape[1]
    tq = q_ref.shape[1]
    lane = lax.broadcasted_iota(jnp.int32, (tq, LANES), 1)
    dn = (((1,), (1,)), ((), ()))

    def attend(nk):
        outs = []
        for a in range(2):
            qh = q_ref[0, :, a * LANES:(a + 1) * LANES]
            kh = k_ref[0, 0:nk, a * LANES:(a + 1) * LANES]
            s = lax.dot_general(qh, kh, dn, preferred_element_type=F32)
            m = jnp.max(s, axis=-1, keepdims=True)
            p = jnp.exp(s - m)
            den = jnp.sum(p, axis=-1, keepdims=True)
            outs.append(jnp.dot(p.astype(BF16), v_ref[0, 0:nk, :], preferred_element_type=F32) / den)
        o_ref[0] = jnp.where(lane < LANES // 2, outs[0], outs[1]).astype(BF16)

    @pl.when(pl.program_id(2) == 0)
    def _():
        attend(ctx_len)

    @pl.when(pl.program_id(2) > 0)
    def _():
        attend(t_total)


def _layer_mla(x, mod, ctx_len, norm_g, w_down, q_norm_g, kv_norm_g, w_uq, w_ukv, qn_g, qr_g, kn_g, kr_g, w_o):
    bsz, t, d = x.shape
    n = t - ctx_len
    h = MLA_HEADS
    pad = LANES - MLA_NOPE - MLA_ROPE
    zc = jnp.zeros((d, MLA_NOPE), F32)
    wd_ext = jnp.concatenate([w_down[:, :MLA_Q_RANK + MLA_KV_RANK], zc, w_down[:, MLA_Q_RANK + MLA_KV_RANK:],
                              jnp.zeros((d, pad), F32)], axis=1).astype(BF16)
    wuq = jnp.pad(w_uq.reshape(MLA_Q_RANK, h, MLA_NOPE + MLA_ROPE), ((0, 0), (0, 0), (0, pad)))
    wuq = wuq.reshape(MLA_Q_RANK, h * LANES).astype(BF16)
    wukv = w_ukv.reshape(MLA_KV_RANK, h, MLA_NOPE + MLA_V)
    wukn = jnp.pad(wukv[:, :, :MLA_NOPE], ((0, 0), (0, 0), (0, LANES - MLA_NOPE))).reshape(MLA_KV_RANK, h * LANES).astype(BF16)
    wuv = wukv[:, :, MLA_NOPE:].reshape(MLA_KV_RANK, h * MLA_V).astype(BF16)
    z_r, z_p = jnp.zeros((MLA_ROPE,), F32), jnp.zeros((pad,), F32)
    gq128 = jnp.concatenate([qn_g, qr_g, z_p]).reshape(1, LANES)
    gk128 = jnp.concatenate([kn_g, z_r, z_p]).reshape(1, LANES)
    gkr128 = jnp.concatenate([jnp.zeros((MLA_NOPE,), F32), kr_g, z_p]).reshape(1, LANES)
    invc = jnp.asarray(np.concatenate([np.full(MLA_NOPE, 1.0 / MLA_NOPE), np.full(MLA_ROPE, 1.0 / MLA_ROPE),
                                       np.ones(pad)]).astype(np.float32)).reshape(1, LANES)
    gm = _block_ones([(MLA_NOPE, True), (MLA_ROPE, True), (pad, False)])
    cos_t, sin_t = _axial_tables(n, ctx_len, MLA_ROPE, MLA_NOPE, 1)
    wdn = MLA_Q_RANK + MLA_KV_RANK + LANES
    q, k, v = pl.pallas_call(
        functools.partial(_mla_proj_kernel, scale=(MLA_NOPE + MLA_ROPE) ** -0.5),
        out_shape=[jax.ShapeDtypeStruct((bsz, t, h * LANES), BF16), jax.ShapeDtypeStruct((bsz, t, h * LANES), BF16),
                   jax.ShapeDtypeStruct((bsz, t, h * MLA_V), BF16)],
        grid=(bsz, t // TM),
        in_specs=[_tok_spec(d), _const_spec((1, d)), _mod_spec(d), _const_spec((d, wdn)),
                  _const_spec((1, MLA_Q_RANK)), _const_spec((1, MLA_KV_RANK)), _const_spec((1, LANES)),
                  _const_spec((MLA_Q_RANK, h * LANES)), _const_spec((MLA_KV_RANK, h * LANES)),
                  _const_spec((MLA_KV_RANK, h * MLA_V)),
                  _const_spec((1, LANES)), _const_spec((1, LANES)), _const_spec((LANES, LANES)),
                  _const_spec((1, LANES)), _tab_spec(), _tab_spec()],
        out_specs=[_tok_spec(h * LANES), _tok_spec(h * LANES), _tok_spec(h * MLA_V)],
        compiler_params=_cparams(("parallel", "parallel")),
        name="mla_proj",
    )(x, norm_g.reshape(1, d), mod, wd_ext, q_norm_g.reshape(1, -1), kv_norm_g.reshape(1, -1), gkr128,
      wuq, wukn, wuv, gq128, gk128, gm, invc, cos_t, sin_t)

    o = pl.pallas_call(
        functools.partial(_mla_attn_kernel, ctx_len=ctx_len),
        out_shape=jax.ShapeDtypeStruct((bsz, t, h * MLA_V), BF16),
        grid=(bsz, h // 2, t // TM),
        in_specs=[pl.BlockSpec((1, TM, 2 * LANES), lambda b, g, j: (b, j, g)),
                  pl.BlockSpec((1, t, 2 * LANES), lambda b, g, j: (b, 0, g)),
                  pl.BlockSpec((1, t, LANES), lambda b, g, j: (b, 0, g))],
        out_specs=pl.BlockSpec((1, TM, LANES), lambda b, g, j: (b, j, g)),
        compiler_params=_cparams(("parallel", "parallel", "arbitrary")),
        name="mla_attn",
    )(q, k, v)
    return _oproj(x, o, mod, w_o.astype(BF16), 2)


def _moe_router_kernel(x_ref, g_ref, mod_ref, wr_ref, h_ref, lg_ref):
    h = _modnorm(x_ref[0], g_ref[...], mod_ref[0, 0, 3:4, :], mod_ref[0, 0, 4:5, :])
    h_ref[0] = h
    lg_ref[0] = lax.dot_general(wr_ref[...], h, (((1,), (1,)), ((), ())),
                                preferred_element_type=F32, precision=HIGHEST)


def _lane_cumsum_excl(x, tri):
    e, w = x.shape
    blk = tri.shape[0]
    carry = jnp.zeros((e, 1), F32)
    outs = []
    for i in range(w // blk):
        xb = x[:, i * blk:(i + 1) * blk]
        inc = jnp.dot(xb.astype(BF16), tri, preferred_element_type=F32)
        outs.append(inc - xb + carry)
        carry = carry + inc[:, blk - 1:blk]
    return outs[0] if len(outs) == 1 else jnp.concatenate(outs, axis=1)


def _moe_select_kernel(lg_ref, tri_ref, idx_ref, gate_ref, *, groups):
    lg = lg_ref[0]
    m = jnp.max(lg, axis=0, keepdims=True)
    ex = jnp.exp(lg - m)
    aff = ex / jnp.sum(ex, axis=0, keepdims=True)
    tri = tri_ref[...]
    ne = lg.shape[0]
    idx_ref[...] = jnp.zeros(idx_ref.shape, jnp.int32)
    gate_ref[...] = jnp.zeros(gate_ref.shape, F32)
    for (t0, tg, cap, s0) in groups:
        a = aff[:, t0:t0 + tg]
        bits = pltpu.bitcast(a, jnp.int32)
        thr = jnp.zeros((ne, 1), jnp.int32)
        for bit in range(30, -1, -1):
            cand = thr | (1 << bit)
            cnt = jnp.sum((bits >= cand).astype(jnp.int32), axis=1, keepdims=True)
            thr = jnp.where(cnt >= cap, cand, thr)
        gt = bits > thr
        eq = (bits == thr).astype(F32)
        need = cap - jnp.sum(gt.astype(jnp.int32), axis=1, keepdims=True)
        eq_rank = _lane_cumsum_excl(eq, tri)
        sel = jnp.where(gt, 1.0, jnp.where(eq_rank < need.astype(F32), eq, 0.0))
        pos = _lane_cumsum_excl(sel, tri)
        pos = jnp.where(sel > 0.0, pos, -1.0)
        tok = t0 + lax.broadcasted_iota(jnp.int32, (1, tg), 1)
        slot = lax.broadcasted_iota(jnp.int32, (cap, tg), 0).astype(F32)
        for e in range(ne):
            onehot = jnp.where(pos[e:e + 1, :] == slot, 1.0, 0.0).astype(BF16)
            ae = a[e:e + 1, :]
            a_hi = ae.astype(BF16)
            r1 = ae - a_hi.astype(F32)
            a_mid = r1.astype(BF16)
            a_lo = (r1 - a_mid.astype(F32)).astype(BF16)
            cols = jnp.concatenate([(tok // 64).astype(F32).astype(BF16), (tok % 64).astype(F32).astype(BF16),
                                    a_hi, a_mid, a_lo, jnp.zeros((3, tg), BF16)], axis=0)
            res = lax.dot_general(cols, onehot, (((1,), (1,)), ((), ())), preferred_element_type=F32)
            idx_ref[0, e:e + 1, s0:s0 + cap] = (res[0:1] * 64.0 + res[1:2]).astype(jnp.int32)
            gate_ref[0, e:e + 1, s0:s0 + cap] = (res[2:3] + res[3:4]) + res[4:5]


def _moe_ffn_kernel(idx_ref, h_hbm, w1_ref, w3_ref, w2_ref, y_ref, hbuf, xin, sem, *, slots):
    b = pl.program_id(0)

    @pl.when(pl.program_id(1) == 0)
    def _():
        cp = pltpu.make_async_copy(h_hbm.at[b], hbuf, sem)
        cp.start()
        cp.wait()

    def gather(c, carry):
        t = idx_ref[0, 0, 0, c]
        xin[pl.ds(c, 1), :] = hbuf[pl.ds(t, 1), :]
        return carry

    lax.fori_loop(0, slots, gather, 0, unroll=8)
    xb = xin[...].astype(BF16)
    h1 = jnp.dot(xb, w1_ref[0], preferred_element_type=F32)
    h3 = jnp.dot(xb, w3_ref[0], preferred_element_type=F32)
    hid = (h1 * _sigmoid(h1) * h3).astype(BF16)
    y_ref[0, 0] = jnp.dot(hid, w2_ref[0], preferred_element_type=F32)


def _moe_combine_kernel(idx_ref, gate_ref, y_ref, mod_ref, x_hbm, out_ref, sem, *, lat_slots, slots):
    b = pl.program_id(0)

    @pl.when(pl.program_id(1) == 0)
    def _():
        cp = pltpu.make_async_copy(x_hbm.at[b], out_ref.at[0], sem)
        cp.start()
        cp.wait()

    def scatter(grp):
        gvec = mod_ref[0, grp, 5:6, :]

        def body(c, carry):
            t = idx_ref[0, 0, 0, c]
            row = y_ref[0, 0, pl.ds(c, 1), :] * gate_ref[0, 0, 0, c]
            out_ref[0, pl.ds(t, 1), :] = out_ref[0, pl.ds(t, 1), :] + gvec * row
            return carry
        return body

    lax.fori_loop(0, lat_slots, scatter(1), 0, unroll=8)
    lax.fori_loop(lat_slots, slots, scatter(0), 0, unroll=8)


def _moe(x, mod, ctx_len, norm_g, w_router, w1_bf, w3_bf, w2_bf):
    bsz, t, d = x.shape
    n = t - ctx_len
    ne = w_router.shape[1]
    ff = w1_bf.shape[2]
    cap_l = max(1, EC_CAPACITY * n // ne)
    cap_c = max(1, EC_CAPACITY * ctx_len // ne)
    slots = cap_l + cap_c
    slot_pad = -(-slots // LANES) * LANES
    h, lg = pl.pallas_call(
        _moe_router_kernel,
        out_shape=[jax.ShapeDtypeStruct((bsz, t, d), F32), jax.ShapeDtypeStruct((bsz, ne, t), F32)],
        grid=(bsz, t // TM),
        in_specs=[_tok_spec(d), _const_spec((1, d)), _mod_spec(d), _const_spec((ne, d))],
        out_specs=[_tok_spec(d), pl.BlockSpec((1, ne, TM), lambda b, i: (b, 0, i))],
        compiler_params=_cparams(("parallel", "parallel")),
        name="moe_router",
    )(x, norm_g.reshape(1, d), mod, w_router.T)

    tri = jnp.asarray(np.triu(np.ones((256, 256), np.float32)), BF16)
    groups = ((ctx_len, n, cap_l, 0), (0, ctx_len, cap_c, cap_l))
    idx, gate = pl.pallas_call(
        functools.partial(_moe_select_kernel, groups=groups),
        out_shape=[jax.ShapeDtypeStruct((bsz, ne, slot_pad), jnp.int32),
                   jax.ShapeDtypeStruct((bsz, ne, slot_pad), F32)],
        grid=(bsz,),
        in_specs=[pl.BlockSpec((1, ne, t), lambda b: (b, 0, 0)), pl.BlockSpec((256, 256), lambda b: (0, 0))],
        out_specs=[pl.BlockSpec((1, ne, slot_pad), lambda b: (b, 0, 0))] * 2,
        compiler_params=_cparams(("parallel",)),
        name="moe_select",
    )(lg, tri)

    smem_spec = pl.BlockSpec((1, 1, 1, slot_pad), lambda b, e: (b, e, 0, 0), memory_space=pltpu.SMEM)
    idx = idx.reshape(bsz, ne, 1, slot_pad)
    gate = gate.reshape(bsz, ne, 1, slot_pad)
    y = pl.pallas_call(
        functools.partial(_moe_ffn_kernel, slots=slots),
        out_shape=jax.ShapeDtypeStruct((bsz, ne, slots, d), F32),
        grid=(bsz, ne),
        in_specs=[smem_spec, pl.BlockSpec(memory_space=pl.ANY),
                  pl.BlockSpec((1, d, ff), lambda b, e: (e, 0, 0)),
                  pl.BlockSpec((1, d, ff), lambda b, e: (e, 0, 0)),
                  pl.BlockSpec((1, ff, d), lambda b, e: (e, 0, 0))],
        out_specs=pl.BlockSpec((1, 1, slots, d), lambda b, e: (b, e, 0, 0)),
        scratch_shapes=[pltpu.VMEM((t, d), F32), pltpu.VMEM((slots, d), F32), pltpu.SemaphoreType.DMA],
        compiler_params=_cparams(("parallel", "arbitrary"), VMEM_BIG),
        name="moe_ffn",
    )(idx, h, w1_bf, w3_bf, w2_bf)

    return pl.pallas_call(
        functools.partial(_moe_combine_kernel, lat_slots=cap_l, slots=slots),
        out_shape=jax.ShapeDtypeStruct((bsz, t, d), F32),
        grid=(bsz, ne),
        in_specs=[smem_spec, smem_spec,
                  pl.BlockSpec((1, 1, slots, d), lambda b, e: (b, e, 0, 0)),
                  pl.BlockSpec((1, 2, 6, d), lambda b, e: (b, 0, 0, 0)),
                  pl.BlockSpec(memory_space=pl.ANY)],
        out_specs=pl.BlockSpec((1, t, d), lambda b, e: (b, 0, 0)),
        scratch_shapes=[pltpu.SemaphoreType.DMA],
        compiler_params=_cparams(("parallel", "arbitrary"), VMEM_BIG),
        name="moe_combine",
    )(idx, gate, y, mod, x)


def _mixer(i, x, mod, ctx_len, p):
    mixer, j = i % 4, i // 4
    if mixer == 0:
        return _layer_na(x, mod, ctx_len, p['norm1_g'][i], p['na_w_qkv'][j], p['na_q_g'][j], p['na_k_g'][j],
                         p['na_rpb'][j], p['na_w_o'][j])
    if mixer == 1:
        return _layer_rwkv(x, mod, ctx_len, p['norm1_g'][i], p['rw_mix'][j], p['rw_w_r'][j], p['rw_w_k'][j],
                           p['rw_w_v'][j], p['rw_w0'][j], p['rw_w1'][j], p['rw_w2'][j], p['rw_a0'][j], p['rw_a1'][j],
                           p['rw_a2'][j], p['rw_g1'][j], p['rw_g2'][j], p['rw_k_k'][j], p['rw_k_a'][j], p['rw_r_k'][j],
                           p['rw_ln_g'][j], p['rw_ln_b'][j], p['rw_w_o'][j])
    if mixer == 2:
        return _layer_mla(x, mod, ctx_len, p['norm1_g'][i], p['mla_w_down'][j], p['mla_q_norm_g'][j],
                          p['mla_kv_norm_g'][j], p['mla_w_uq'][j], p['mla_w_ukv'][j], p['mla_qn_g'][j],
                          p['mla_qr_g'][j], p['mla_kn_g'][j], p['mla_kr_g'][j], p['mla_w_o'][j])
    if mixer == 3:
        return _layer_swa(x, mod, ctx_len, p['norm1_g'][i], p['swa_w_qkv'][j], p['swa_q_g'][j], p['swa_k_g'][j],
                          p['swa_sink'][j], p['swa_w_o'][j])
    return x


_PARAM_NAMES = (
    'x', 'c', 'ctx', 'c_ctx', 'norm1_g', 'norm2_g', 'ada_w', 'ada_b',
    'na_w_qkv', 'na_q_g', 'na_k_g', 'na_rpb', 'na_w_o',
    'rw_mix', 'rw_w_r', 'rw_w_k', 'rw_w_v', 'rw_w0', 'rw_w1', 'rw_w2', 'rw_a0', 'rw_a1', 'rw_a2',
    'rw_g1', 'rw_g2', 'rw_k_k', 'rw_k_a', 'rw_r_k', 'rw_ln_g', 'rw_ln_b', 'rw_w_o',
    'mla_w_down', 'mla_q_norm_g', 'mla_kv_norm_g', 'mla_w_uq', 'mla_w_ukv',
    'mla_qn_g', 'mla_qr_g', 'mla_kn_g', 'mla_kr_g', 'mla_w_o',
    'swa_w_qkv', 'swa_q_g', 'swa_k_g', 'swa_sink', 'swa_w_o',
    'moe_router', 'moe_w1', 'moe_w3', 'moe_w2')


def kernel(*args):
    p = dict(zip(_PARAM_NAMES, args))
    ctx_len = p['ctx'].shape[1]
    assert ctx_len == TM
    depth = p['ada_w'].shape[0]
    mods = _ada_all(p['c'], p['c_ctx'], p['ada_w'], p['ada_b'])
    x = jnp.concatenate([p['ctx'], p['x']], axis=1)
    for i in range(depth):
        x = _mixer(i, x, mods[i], ctx_len, p)
        x = _moe(x, mods[i], ctx_len, p['norm2_g'][i], p['moe_router'][i], p['moe_w1'][i].astype(BF16),
                 p['moe_w3'][i].astype(BF16), p['moe_w2'][i].astype(BF16))
    return x[:, ctx_len:]
```
